```python
import jax, jax.numpy as jnp
from jax import lax
import numpy as np

D_MODEL = 4096
BATCH = 4
SEQ = 2048
DEPTH = 2
DEC_BATCH = 16
DEC_SEQ = 32
PAST_LEN = 4096

CHUNK = 64
BAND_PREV = 8
BAND_LEN = BAND_PREV * CHUNK
D_ATTN = D_MODEL // 2
HEAD_DIM = 128
N_HEADS = D_ATTN // HEAD_DIM
MAX_REL = 128
D_POOL = D_MODEL // 4
POOL_WINDOWS = (2, 4, 8, 16)
POOL_GROUPS = len(POOL_WINDOWS)
POOL_GC = D_POOL // POOL_GROUPS
POOL_OUT = D_MODEL // POOL_GROUPS
POOL_HIST = max(POOL_WINDOWS) - 1
N_MEM = 256
MEM_HEADS = 4
D_MEM = D_MODEL // 4
MEM_HEAD_DIM = D_MEM // MEM_HEADS
N_BRANCHES = 3
IN_COLS = 3 * D_ATTN + D_POOL + D_MEM
IN_SPLITS = (D_ATTN, 2 * D_ATTN, 3 * D_ATTN, 3 * D_ATTN + D_POOL)
N_EXPERTS = 32
TOP_K = 4
D_FF = D_MODEL
SWIGLU_LIMIT = 7.0
SWIGLU_ALPHA = 1.702
ALPHA = (2 * DEPTH) ** 0.25
BETA = (8 * DEPTH) ** -0.25
LN_EPS = 1e-5
NEG_INF = -1e30

kernel_name = "hybrid_streaming_encoder_step"


def _normal(key, shape, scale):
    return jax.random.normal(key, shape, jnp.float32) * scale


def _layer_normal(key, shape, scale):
    keys = jax.random.split(key, DEPTH)
    return jax.vmap(lambda k: jax.random.normal(k, shape, jnp.float32) * scale)(keys)


def setup_inputs(seed: int = 0) -> dict:
    key = jax.random.key(seed)
    k = jax.random.split(key, 32)
    d = D_MODEL
    band_len = min(BAND_LEN, PAST_LEN)
    x_prompt = _normal(k[0], (BATCH, SEQ, d), 1.0)
    x_sample = _normal(k[1], (DEC_BATCH, DEC_SEQ, d), 1.0)
    mem_prompt = _normal(k[2], (BATCH, N_MEM, d), 1.0)
    cache_band_k = _normal(k[3], (DEPTH, DEC_BATCH, band_len, N_HEADS, HEAD_DIM), 1.0)
    cache_band_v = _normal(k[4], (DEPTH, DEC_BATCH, band_len, N_HEADS, HEAD_DIM), BETA)
    state_pool = _normal(k[5], (DEPTH, DEC_BATCH, POOL_HIST, D_POOL), 1.0)
    cache_mem_k = _normal(k[6], (DEPTH, DEC_BATCH, N_MEM, MEM_HEADS, MEM_HEAD_DIM), 1.0)
    cache_mem_v = _normal(k[7], (DEPTH, DEC_BATCH, N_MEM, MEM_HEADS, MEM_HEAD_DIM), BETA)
    w_in = jnp.concatenate([
        _layer_normal(k[8], (d, 2 * D_ATTN), d ** -0.5),
        _layer_normal(k[9], (d, D_ATTN), BETA * d ** -0.5),
        _layer_normal(k[10], (d, D_POOL + D_MEM), d ** -0.5),
    ], axis=-1)
    rel_bias = _layer_normal(k[11], (N_HEADS, 2 * MAX_REL + 1), 0.5)
    w_attn_proj = _layer_normal(k[12], (D_ATTN, d), D_ATTN ** -0.5)
    w_pool_map = _layer_normal(k[13], (POOL_GROUPS, POOL_GC, POOL_OUT), POOL_GC ** -0.5)
    pool_scale = 1.0 + _layer_normal(k[14], (d,), 0.1)
    w_mem_kv = jnp.concatenate([
        _layer_normal(k[15], (d, D_MEM), d ** -0.5),
        _layer_normal(k[16], (d, D_MEM), BETA * d ** -0.5),
    ], axis=-1)
    w_mem_proj = _layer_normal(k[17], (D_MEM, d), D_MEM ** -0.5)
    w_gate = _layer_normal(k[18], (d, N_BRANCHES * d), d ** -0.5)
    b_gate = _layer_normal(k[19], (N_BRANCHES * d,), 0.1)
    w_out = _layer_normal(k[20], (d, d), BETA * d ** -0.5)
    ln_mix_g = 1.0 + _layer_normal(k[21], (d,), 0.05)
    ln_mix_b = _layer_normal(k[22], (d,), 0.02)
    w_router = _layer_normal(k[23], (d, N_EXPERTS), d ** -0.5)
    b_router = _layer_normal(k[24], (N_EXPERTS,), 0.01)
    w_exp_in = _layer_normal(k[25], (N_EXPERTS, d, 2 * D_FF), d ** -0.5)
    b_exp_in = _layer_normal(k[26], (N_EXPERTS, 2 * D_FF), 0.02)
    w_exp_out = _layer_normal(k[27], (N_EXPERTS, D_FF, d), BETA * D_FF ** -0.5)
    b_exp_out = _layer_normal(k[28], (N_EXPERTS, d), 0.02)
    ln_ffn_g = 1.0 + _layer_normal(k[29], (d,), 0.05)
    ln_ffn_b = _layer_normal(k[30], (d,), 0.02)
    return {
        "x_prompt": x_prompt, "x_sample": x_sample, "mem_prompt": mem_prompt,
        "cache_band_k": cache_band_k, "cache_band_v": cache_band_v, "state_pool": state_pool,
        "cache_mem_k": cache_mem_k, "cache_mem_v": cache_mem_v,
        "w_in": w_in, "rel_bias": rel_bias, "w_attn_proj": w_attn_proj,
        "w_pool_map": w_pool_map, "pool_scale": pool_scale,
        "w_mem_kv": w_mem_kv, "w_mem_proj": w_mem_proj,
        "w_gate": w_gate, "b_gate": b_gate, "w_out": w_out,
        "ln_mix_g": ln_mix_g, "ln_mix_b": ln_mix_b,
        "w_router": w_router, "b_router": b_router,
        "w_exp_in": w_exp_in, "b_exp_in": b_exp_in, "w_exp_out": w_exp_out, "b_exp_out": b_exp_out,
        "ln_ffn_g": ln_ffn_g, "ln_ffn_b": ln_ffn_b,
    }


def layer_norm(x, g, b):
    xf = x.astype(jnp.float32)
    mu = jnp.mean(xf, axis=-1, keepdims=True)
    xc = xf - mu
    var = jnp.mean(xc * xc, axis=-1, keepdims=True)
    return (xc * lax.rsqrt(var + LN_EPS) * g + b).astype(x.dtype)


def chunk_band_softmax(q, k_band, v_band, q_pos, k_pos, rel_bias):
    s = jnp.einsum('bnqhd,bnkhd->bnhqk', q, k_band).astype(jnp.float32) * HEAD_DIM ** -0.5
    rel = jnp.clip(q_pos[:, :, None] - k_pos[:, None, :], -MAX_REL, MAX_REL) + MAX_REL
    bias = jnp.transpose(rel_bias[:, rel], (1, 0, 2, 3)).astype(jnp.float32)
    valid = (k_pos >= 0)[:, None, None, :]
    s = jnp.where(valid, s + bias, NEG_INF)
    p = jax.nn.softmax(s, axis=-1).astype(v_band.dtype)
    return jnp.einsum('bnhqk,bnkhd->bnqhd', p, v_band)


def band_attention_prompt(q, k, v, rel_bias):
    bsz, t = q.shape[:2]
    nc = t // CHUNK
    qc = q.reshape(bsz, nc, CHUNK, N_HEADS, HEAD_DIM)

    def gather_band(a):
        ac = a.reshape(bsz, nc, CHUNK, N_HEADS, HEAD_DIM)
        ap = jnp.pad(ac, ((0, 0), (BAND_PREV, 0), (0, 0), (0, 0), (0, 0)))
        return jnp.concatenate([ap[:, i:i + nc] for i in range(BAND_PREV + 1)], axis=2)

    chunk_ids = jnp.arange(nc)[:, None]
    q_pos = chunk_ids * CHUNK + jnp.arange(CHUNK)[None]
    k_pos = (chunk_ids - BAND_PREV) * CHUNK + jnp.arange(BAND_LEN + CHUNK)[None]
    o = chunk_band_softmax(qc, gather_band(k), gather_band(v), q_pos, k_pos, rel_bias)
    return o.reshape(bsz, t, N_HEADS, HEAD_DIM)


def band_attention_cached(cache_k, cache_v):
    def attend(q, k, v, rel_bias):
        t = q.shape[1]
        n_cached = cache_k.shape[1]
        k_band = jnp.concatenate([cache_k.astype(k.dtype), k], axis=1)[:, None]
        v_band = jnp.concatenate([cache_v.astype(v.dtype), v], axis=1)[:, None]
        q_pos = (PAST_LEN + jnp.arange(t))[None]
        k_pos = (PAST_LEN - n_cached + jnp.arange(n_cached + t))[None]
        return chunk_band_softmax(q[:, None], k_band, v_band, q_pos, k_pos, rel_bias)[:, 0]
    return attend


def multiscale_pool(u, hist, pos0, w_pool_map, pool_scale):
    t = u.shape[1]
    ext = jnp.concatenate([hist.astype(u.dtype), u], axis=1)
    cs = jnp.pad(jnp.cumsum(ext.astype(jnp.float32), axis=1), ((0, 0), (1, 0), (0, 0)))
    pos = pos0 + jnp.arange(t)
    end = cs[:, POOL_HIST + 1:POOL_HIST + 1 + t]
    outs = []
    for g, w in enumerate(POOL_WINDOWS):
        c = slice(g * POOL_GC, (g + 1) * POOL_GC)
        start = cs[:, POOL_HIST + 1 - w:POOL_HIST + 1 - w + t, c]
        cnt = jnp.minimum(w, pos + 1).astype(jnp.float32)[None, :, None]
        mixed = (end[..., c] - start) / cnt - u[..., c].astype(jnp.float32)
        outs.append(mixed.astype(u.dtype) @ w_pool_map[g])
    return jnp.concatenate(outs, axis=-1) * pool_scale, ext[:, -POOL_HIST:]


def memory_attention(q, mem_k, mem_v):
    s = jnp.einsum('bthd,bmhd->bhtm', q, mem_k.astype(q.dtype)).astype(jnp.float32) * MEM_HEAD_DIM ** -0.5
    p = jax.nn.softmax(s, axis=-1).astype(q.dtype)
    return jnp.einsum('bhtm,bmhd->bthd', p, mem_v.astype(q.dtype))


def moe_ffn(x, w_router, b_router, w_exp_in, b_exp_in, w_exp_out, b_exp_out):
    bsz, t, d = x.shape
    xt = x.reshape(bsz * t, d)
    logits = (xt @ w_router + b_router).astype(jnp.float32)
    top_val, top_idx = lax.top_k(logits, TOP_K)
    top_w = jax.nn.softmax(top_val, axis=-1)
    combine = jnp.sum(jax.nn.one_hot(top_idx, N_EXPERTS, dtype=jnp.float32) * top_w[..., None], axis=1)
    out = jnp.zeros_like(xt)
    for e in range(N_EXPERTS):
        h = xt @ w_exp_in[e] + b_exp_in[e]
        gate, up = h[:, :D_FF], h[:, D_FF:]
        gate = jnp.minimum(gate, SWIGLU_LIMIT)
        up = jnp.clip(up, -SWIGLU_LIMIT, SWIGLU_LIMIT)
        glu = gate * jax.nn.sigmoid(SWIGLU_ALPHA * gate)
        y = (glu * (up + 1.0)) @ w_exp_out[e] + b_exp_out[e]
        out = out + combine[:, e:e + 1].astype(x.dtype) * y
    return out.reshape(bsz, t, d)


def trunk_layer(x, attend, pool_hist, pos0, mem_k, mem_v,
                w_in, rel_bias, w_attn_proj, w_pool_map, pool_scale, w_mem_proj,
                w_gate, b_gate, w_out, ln_mix_g, ln_mix_b,
                w_router, b_router, w_exp_in, b_exp_in, w_exp_out, b_exp_out, ln_ffn_g, ln_ffn_b):
    bsz, t, _ = x.shape
    h = x @ w_in
    q_a, k_a, v_a, u, q_m = jnp.split(h, IN_SPLITS, axis=-1)
    q_a = q_a.reshape(bsz, t, N_HEADS, HEAD_DIM)
    k_a = k_a.reshape(bsz, t, N_HEADS, HEAD_DIM)
    v_a = v_a.reshape(bsz, t, N_HEADS, HEAD_DIM)
    br_attn = attend(q_a, k_a, v_a, rel_bias).reshape(bsz, t, D_ATTN) @ w_attn_proj
    br_pool, new_pool = multiscale_pool(u, pool_hist, pos0, w_pool_map, pool_scale)
    br_mem = memory_attention(q_m.reshape(bsz, t, MEM_HEADS, MEM_HEAD_DIM), mem_k, mem_v)
    br_mem = br_mem.reshape(bsz, t, D_MEM) @ w_mem_proj
    gates = jax.nn.sigmoid(x @ w_gate + b_gate).reshape(bsz, t, N_BRANCHES, D_MODEL)
    merged = gates[:, :, 0] * br_attn + gates[:, :, 1] * br_pool + gates[:, :, 2] * br_mem
    x = layer_norm(ALPHA * x + merged @ w_out, ln_mix_g, ln_mix_b)
    x = layer_norm(ALPHA * x + moe_ffn(x, w_router, b_router, w_exp_in, b_exp_in, w_exp_out, b_exp_out),
                   ln_ffn_g, ln_ffn_b)
    return x, k_a, v_a, new_pool


def reference(x_prompt, x_sample, mem_prompt, cache_band_k, cache_band_v, state_pool,
              cache_mem_k, cache_mem_v, w_in, rel_bias, w_attn_proj, w_pool_map, pool_scale,
              w_mem_kv, w_mem_proj, w_gate, b_gate, w_out, ln_mix_g, ln_mix_b,
              w_router, b_router, w_exp_in, b_exp_in, w_exp_out, b_exp_out, ln_ffn_g, ln_ffn_b):
    yp, ys = x_prompt, x_sample
    bsz_p = x_prompt.shape[0]
    band_k_p, band_v_p, pool_p, mem_k_p, mem_v_p = [], [], [], [], []
    band_k_s, band_v_s, pool_s = [], [], []
    for l in range(DEPTH):
        weights = (w_in[l], rel_bias[l], w_attn_proj[l], w_pool_map[l], pool_scale[l], w_mem_proj[l],
                   w_gate[l], b_gate[l], w_out[l], ln_mix_g[l], ln_mix_b[l],
                   w_router[l], b_router[l], w_exp_in[l], b_exp_in[l], w_exp_out[l], b_exp_out[l],
                   ln_ffn_g[l], ln_ffn_b[l])
        mem_kv = mem_prompt @ w_mem_kv[l]
        mk = mem_kv[..., :D_MEM].reshape(bsz_p, N_MEM, MEM_HEADS, MEM_HEAD_DIM)
        mv = mem_kv[..., D_MEM:].reshape(bsz_p, N_MEM, MEM_HEADS, MEM_HEAD_DIM)
        zero_hist = jnp.zeros((bsz_p, POOL_HIST, D_POOL), yp.dtype)
        yp, k_p, v_p, new_pool_p = trunk_layer(yp, band_attention_prompt, zero_hist, 0, mk, mv, *weights)
        keep = min(BAND_LEN, yp.shape[1])
        band_k_p.append(k_p[:, -keep:])
        band_v_p.append(v_p[:, -keep:])
        pool_p.append(new_pool_p)
        mem_k_p.append(mk)
        mem_v_p.append(mv)
        ys, k_s, v_s, new_pool_s = trunk_layer(
            ys, band_attention_cached(cache_band_k[l], cache_band_v[l]), state_pool[l], PAST_LEN,
            cache_mem_k[l], cache_mem_v[l], *weights)
        band_k_s.append(k_s)
        band_v_s.append(v_s)
        pool_s.append(new_pool_s)
    return (yp, ys,
            jnp.stack(band_k_p), jnp.stack(band_v_p), jnp.stack(pool_p),
            jnp.stack(mem_k_p), jnp.stack(mem_v_p),
            jnp.stack(band_k_s), jnp.stack(band_v_s), jnp.stack(pool_s))
```

```python
import functools

import numpy as np
import jax
import jax.numpy as jnp
from jax import lax
from jax.experimental import pallas as pl
from jax.experimental.pallas import tpu as pltpu

CHUNK = 64
BAND_PREV = 8
HEAD_DIM = 128
POOL_WINDOWS = (2, 4, 8, 16)
POOL_HIST = max(POOL_WINDOWS) - 1
HIST_ROWS = POOL_HIST + 1
MEM_HEADS = 4
N_BRANCHES = 3
TOP_K = 4
SWIGLU_LIMIT = 7.0
SWIGLU_ALPHA = 1.702
LN_EPS = 1e-5
NEG_INF = -1e30
PAST_LEN = 4096

ROW_TILE = 256
MIB = 1024 * 1024

F32 = jnp.float32
BF16 = jnp.bfloat16


def _pick(n, cands):
    for c in cands:
        if n % c == 0:
            return c
    return n


def _params(n_axes, vmem_mib):
    return pltpu.CompilerParams(
        dimension_semantics=("arbitrary",) * n_axes, vmem_limit_bytes=int(vmem_mib * MIB))


def _dot(a, b):
    return jnp.dot(a, b, preferred_element_type=F32)


def _dot_t(a, b):
    return lax.dot_general(a, b, (((1,), (1,)), ((), ())), preferred_element_type=F32)


def _mm_body(a_ref, w_ref, o_ref):
    o_ref[...] = _dot(a_ref[...], w_ref[...]).astype(o_ref.dtype)


def _matmul(a, w, out_dtype, name):
    m, k = a.shape
    n = w.shape[1]
    tm = _pick(m, (1088, 1024, 512, 256, 128))
    tn = _pick(n, (1024, 512, 256, 128))
    return pl.pallas_call(
        _mm_body,
        grid=(m // tm, n // tn),
        in_specs=[pl.BlockSpec((tm, k), lambda i, j: (i, 0)),
                  pl.BlockSpec((k, tn), lambda i, j: (0, j))],
        out_specs=pl.BlockSpec((tm, tn), lambda i, j: (i, j)),
        out_shape=jax.ShapeDtypeStruct((m, n), out_dtype),
        compiler_params=_params(2, 56),
        name=name,
    )(a, w)


def _softmax_pv(scores, values):
    m = scores[0].max(axis=-1, keepdims=True)
    for s in scores[1:]:
        m = jnp.maximum(m, s.max(axis=-1, keepdims=True))
    es = [jnp.exp(s - m) for s in scores]
    l = es[0].sum(axis=-1, keepdims=True)
    for e in es[1:]:
        l = l + e.sum(axis=-1, keepdims=True)
    inv = 1.0 / l
    o = None
    for e, v in zip(es, values):
        oj = _dot((e * inv).astype(BF16), v)
        o = oj if o is None else o + oj
    return o


def _band_prompt_body(*refs, heads, nkb, qrows, scale):
    q_ref = refs[0]
    k_refs = refs[1:1 + nkb]
    v_refs = refs[1 + nkb:1 + 2 * nkb]
    bias_ref = refs[1 + 2 * nkb]
    o_ref = refs[2 + 2 * nkb]
    qb = pl.program_id(2)
    for h in range(heads):
        sl = slice(h * HEAD_DIM, (h + 1) * HEAD_DIM)
        q = q_ref[:, sl].astype(BF16)
        scores = []
        for j in range(nkb):
            s = _dot_t(q, k_refs[j][:, sl].astype(BF16)) * scale
            s = s + bias_ref[h, :, j * qrows:(j + 1) * qrows]
            if j < nkb - 1:
                s = jnp.where(qb >= nkb - 1 - j, s, NEG_INF)
            scores.append(s)
        o = _softmax_pv(scores, [v_refs[j][:, sl].astype(BF16) for j in range(nkb)])
        o_ref[:, sl] = o.astype(o_ref.dtype)


def _band_prompt(h, bias, *, batch, seq, d_attn, heads_per_step):
    qrows = ROW_TILE
    nkb = BAND_PREV * CHUNK // qrows + 1
    nq = seq // qrows
    gw = heads_per_step * HEAD_DIM
    hg = d_attn // gw
    q_spec = pl.BlockSpec((qrows, gw), lambda g, b, t: (b * nq + t, g))

    def kv_spec(j, col0):
        back = nkb - 1 - j
        return pl.BlockSpec((qrows, gw),
                            lambda g, b, t: (b * nq + jnp.maximum(t - back, 0), col0 + g))

    in_specs = ([q_spec] + [kv_spec(j, hg) for j in range(nkb)]
                + [kv_spec(j, 2 * hg) for j in range(nkb)]
                + [pl.BlockSpec((heads_per_step, qrows, nkb * qrows), lambda g, b, t: (g, 0, 0))])
    body = functools.partial(_band_prompt_body, heads=heads_per_step, nkb=nkb, qrows=qrows,
                             scale=HEAD_DIM ** -0.5)
    return pl.pallas_call(
        body,
        grid=(hg, batch, nq),
        in_specs=in_specs,
        out_specs=pl.BlockSpec((qrows, gw), lambda g, b, t: (b * nq + t, g)),
        out_shape=jax.ShapeDtypeStruct((batch * seq, d_attn), BF16),
        compiler_params=_params(3, 40),
        name="band_prompt",
    )(*([h] * (1 + 2 * nkb)), bias)


def _band_sample_body(q_ref, kc_ref, kn_ref, vc_ref, vn_ref, bc_ref, bn_ref, o_ref, *, heads, scale):
    for h in range(heads):
        sl = slice(h * HEAD_DIM, (h + 1) * HEAD_DIM)
        q = q_ref[:, sl].astype(BF16)
        sc = _dot_t(q, kc_ref[:, sl].astype(BF16)) * scale + bc_ref[h]
        sn = _dot_t(q, kn_ref[:, sl].astype(BF16)) * scale + bn_ref[h]
        o = _softmax_pv([sc, sn], [vc_ref[:, sl].astype(BF16), vn_ref[:, sl].astype(BF16)])
        o_ref[:, sl] = o.astype(o_ref.dtype)


def _band_sample(h, cache_k, cache_v, bias_c, bias_n, *, row0, batch, seq, d_attn, heads_per_step):
    n_cached = cache_k.shape[1]
    gw = heads_per_step * HEAD_DIM
    hg = d_attn // gw
    rb0 = row0 // seq
    new_spec = lambda col0: pl.BlockSpec((seq, gw), lambda g, b: (rb0 + b, col0 + g))
    cache_spec = pl.BlockSpec((None, n_cached, gw), lambda g, b: (b, 0, g))
    body = functools.partial(_band_sample_body, heads=heads_per_step, scale=HEAD_DIM ** -0.5)
    return pl.pallas_call(
        body,
        grid=(hg, batch),
        in_specs=[new_spec(0), cache_spec, new_spec(hg), cache_spec, new_spec(2 * hg),
                  pl.BlockSpec((heads_per_step, seq, n_cached), lambda g, b: (g, 0, 0)),
                  pl.BlockSpec((heads_per_step, seq, seq), lambda g, b: (g, 0, 0))],
        out_specs=pl.BlockSpec((seq, gw), lambda g, b: (b, g)),
        out_shape=jax.ShapeDtypeStruct((batch * seq, d_attn), BF16),
        compiler_params=_params(2, 40),
        name="band_sample",
    )(h, cache_k, h, cache_v, h, bias_c, bias_n)


def _band_bias_prompt(rel_bias, qrows, nkb):
    max_rel = (rel_bias.shape[1] - 1) // 2
    i = np.arange(qrows)[:, None]
    j = np.arange(nkb * qrows)[None, :]
    dist = i + (nkb - 1) * qrows - j
    rel = np.clip(dist, -max_rel, max_rel) + max_rel
    cq, ck = i // CHUNK, j // CHUNK
    in_band = (ck >= cq) & (ck <= cq + BAND_PREV)
    return jnp.where(jnp.asarray(in_band)[None], rel_bias[:, jnp.asarray(rel)], NEG_INF).astype(F32)


def _band_bias_sample(rel_bias, seq, n_cached):
    max_rel = (rel_bias.shape[1] - 1) // 2
    i = np.arange(seq)[:, None]
    rel_c = np.clip(i + n_cached - np.arange(n_cached)[None, :], -max_rel, max_rel) + max_rel
    rel_n = np.clip(i - np.arange(seq)[None, :], -max_rel, max_rel) + max_rel
    return (rel_bias[:, jnp.asarray(rel_c)].astype(F32), rel_bias[:, jnp.asarray(rel_n)].astype(F32))


def _mem_body(q_ref, k_ref, v_ref, o_ref, *, heads, hd, scale):
    for h in range(heads):
        sl = slice(h * hd, (h + 1) * hd)
        s = _dot_t(q_ref[:, sl].astype(BF16), k_ref[:, sl].astype(BF16)) * scale
        o = _softmax_pv([s], [v_ref[:, sl].astype(BF16)])
        o_ref[:, sl] = o.astype(o_ref.dtype)


def _mem_attn(h, kmat, vmat, *, row0, batch, seq, d_mem, q_col_blk, k_col_blk, v_col_blk, n_mem, name):
    tq = _pick(seq, (512, 256, 128, 64, 32))
    nt = seq // tq
    rb0 = row0 // tq
    hd = d_mem // MEM_HEADS
    body = functools.partial(_mem_body, heads=MEM_HEADS, hd=hd, scale=hd ** -0.5)
    return pl.pallas_call(
        body,
        grid=(batch, nt),
        in_specs=[pl.BlockSpec((tq, d_mem), lambda b, t: (rb0 + b * nt + t, q_col_blk)),
                  pl.BlockSpec((n_mem, d_mem), lambda b, t: (b, k_col_blk)),
                  pl.BlockSpec((n_mem, d_mem), lambda b, t: (b, v_col_blk))],
        out_specs=pl.BlockSpec((tq, d_mem), lambda b, t: (b * nt + t, 0)),
        out_shape=jax.ShapeDtypeStruct((batch * seq, d_mem), BF16),
        compiler_params=_params(2, 40),
        name=name,
    )(h, kmat, vmat)


def _pool_body(u_ref, prev_ref, o_ref, ext_ref, *, rows, gc, pos0, zero_first_hist):
    t = pl.program_id(1)
    prev = prev_ref[...]
    if zero_first_hist:
        prev = jnp.where(t == 0, 0.0, prev)
    ext_ref[0:HIST_ROWS, :] = prev
    ext_ref[HIST_ROWS:HIST_ROWS + rows, :] = u_ref[...]
    pos = pos0 + t * rows + lax.broadcasted_iota(jnp.int32, (rows, 1), 0)
    for g, w in enumerate(POOL_WINDOWS):
        cs = slice(g * gc, (g + 1) * gc)
        acc = ext_ref[HIST_ROWS:HIST_ROWS + rows, cs]
        for back in range(1, w):
            acc = acc + ext_ref[HIST_ROWS - back:HIST_ROWS - back + rows, cs]
        cnt = jnp.minimum(w, pos + 1).astype(F32)
        o_ref[:, cs] = (acc / cnt - u_ref[:, cs]).astype(o_ref.dtype)


def _pool(h, hist, *, row0, batch, seq, d_pool, u_col_blk, pos0, name):
    rows = _pick(seq, (ROW_TILE, 128, 64, 32))
    nt = seq // rows
    rb0 = row0 // rows
    gc = d_pool // len(POOL_WINDOWS)
    if hist is None:
        per = rows // HIST_ROWS
        prev_arr = h
        prev_spec = pl.BlockSpec(
            (HIST_ROWS, d_pool),
            lambda b, t: (jnp.maximum((rb0 + b * nt + t) * per - 1, 0), u_col_blk))
    else:
        assert nt == 1
        prev_arr = hist
        prev_spec = pl.BlockSpec((HIST_ROWS, d_pool), lambda b, t: (b, 0))
    body = functools.partial(_pool_body, rows=rows, gc=gc, pos0=pos0, zero_first_hist=hist is None)
    return pl.pallas_call(
        body,
        grid=(batch, nt),
        in_specs=[pl.BlockSpec((rows, d_pool), lambda b, t: (rb0 + b * nt + t, u_col_blk)), prev_spec],
        out_specs=pl.BlockSpec((rows, d_pool), lambda b, t: (b * nt + t, 0)),
        out_shape=jax.ShapeDtypeStruct((batch * seq, d_pool), BF16),
        scratch_shapes=[pltpu.VMEM((HIST_ROWS + rows, d_pool), F32)],
        compiler_params=_params(2, 32),
        name=name,
    )(h, prev_arr)


def _merge_body(x_ref, wg0, wg1, wg2, bg0, bg1, bg2, a_ref, wa_ref, p_ref, wp_ref, ps_ref,
                m_ref, wm_ref, o_ref):
    x = x_ref[...]

    def gate(w_ref, b_ref):
        return jax.nn.sigmoid(_dot(x, w_ref[...]) + b_ref[...])

    br_attn = _dot(a_ref[...], wa_ref[...])
    br_pool = _dot(p_ref[...], wp_ref[...]) * ps_ref[...]
    br_mem = _dot(m_ref[...], wm_ref[...])
    merged = gate(wg0, bg0) * br_attn + gate(wg1, bg1) * br_pool + gate(wg2, bg2) * br_mem
    o_ref[...] = merged.astype(o_ref.dtype)


def _merge(xb, w_gate, b_gate, attn, w_attn, mixed, w_pool, pool_scale, mem, w_mem):
    m, d = xb.shape
    d_attn, d_pool, d_mem = attn.shape[1], mixed.shape[1], mem.shape[1]
    gc = d_pool // len(POOL_WINDOWS)
    pool_out = w_pool.shape[1]
    tm = _pick(m, (512, 256, 128))
    tn = _pick(pool_out, (256, 128))
    nj = d // tn
    per_group = pool_out // tn
    row = lambda i, j: (i, 0)
    gate_w = lambda br: pl.BlockSpec((d, tn), lambda i, j: (0, br * nj + j))
    gate_b = lambda br: pl.BlockSpec((1, tn), lambda i, j: (0, br * nj + j))
    return pl.pallas_call(
        _merge_body,
        grid=(m // tm, nj),
        in_specs=[pl.BlockSpec((tm, d), row), gate_w(0), gate_w(1), gate_w(2),
                  gate_b(0), gate_b(1), gate_b(2),
                  pl.BlockSpec((tm, d_attn), row), pl.BlockSpec((d_attn, tn), lambda i, j: (0, j)),
                  pl.BlockSpec((tm, gc), lambda i, j: (i, j // per_group)),
                  pl.BlockSpec((gc, tn), lambda i, j: (j // per_group, j % per_group)),
                  pl.BlockSpec((1, tn), lambda i, j: (0, j)),
                  pl.BlockSpec((tm, d_mem), row), pl.BlockSpec((d_mem, tn), lambda i, j: (0, j))],
        out_specs=pl.BlockSpec((tm, tn), lambda i, j: (i, j)),
        out_shape=jax.ShapeDtypeStruct((m, d), BF16),
        compiler_params=_params(2, 48),
        name="merge",
    )(xb, w_gate, w_gate, w_gate, b_gate, b_gate, b_gate, attn, w_attn, mixed, w_pool, pool_scale,
      mem, w_mem)


def _layer_norm_rows(y, g, b):
    mu = jnp.mean(y, axis=-1, keepdims=True)
    yc = y - mu
    var = jnp.mean(yc * yc, axis=-1, keepdims=True)
    return yc * lax.rsqrt(var + LN_EPS) * g + b


def _outln_body(mg_ref, w_ref, x_ref, g_ref, b_ref, wr_ref, br_ref, x1_ref, idx_ref, wgt_ref, acc_ref,
                *, nj, tn, alpha, n_exp):
    j = pl.program_id(1)
    acc_ref[j] = _dot(mg_ref[...], w_ref[...])

    @pl.when(j == nj - 1)
    def _():
        tm = x_ref.shape[0]
        d = nj * tn
        tot = jnp.zeros((tm, 1), F32)
        for jj in range(nj):
            y = alpha * x_ref[:, jj * tn:(jj + 1) * tn] + acc_ref[jj]
            acc_ref[jj] = y
            tot = tot + y.sum(axis=-1, keepdims=True)
        mu = tot / d
        sq = jnp.zeros((tm, 1), F32)
        for jj in range(nj):
            yc = acc_ref[jj] - mu
            sq = sq + (yc * yc).sum(axis=-1, keepdims=True)
        rstd = lax.rsqrt(sq / d + LN_EPS)
        logits = jnp.zeros((tm, n_exp), F32)
        for jj in range(nj):
            sl = slice(jj * tn, (jj + 1) * tn)
            xn = (acc_ref[jj] - mu) * rstd * g_ref[:, sl] + b_ref[:, sl]
            x1_ref[:, sl] = xn
            logits = logits + _dot(xn.astype(BF16), wr_ref[sl, :].astype(BF16))
        logits = logits + br_ref[...]

        col = lax.broadcasted_iota(jnp.int32, (tm, n_exp), 1).astype(F32)
        vals, idxs = [], []
        for _ in range(TOP_K):
            mx = logits.max(axis=-1, keepdims=True)
            ix = jnp.where(logits == mx, col, float(n_exp)).min(axis=-1, keepdims=True)
            vals.append(mx)
            idxs.append(ix)
            logits = jnp.where(col == ix, -jnp.inf, logits)
        es = [jnp.exp(v - vals[0]) for v in vals]
        den = es[0]
        for e in es[1:]:
            den = den + e
        lane = lax.broadcasted_iota(jnp.int32, idx_ref.shape, 1)
        idx_out = jnp.zeros(idx_ref.shape, F32)
        wgt_out = jnp.zeros(wgt_ref.shape, F32)
        for k in range(TOP_K):
            idx_out = jnp.where(lane == k, idxs[k], idx_out)
            wgt_out = jnp.where(lane == k, es[k] / den, wgt_out)
        idx_ref[...] = idx_out.astype(jnp.int32)
        wgt_ref[...] = wgt_out


def _outln_router(merged, w_out, x, ln_g, ln_b, w_router, b_router, *, alpha):
    m, d = x.shape
    n_exp = w_router.shape[1]
    tm = ROW_TILE
    tn = _pick(d, (512, 256, 128))
    nj = d // tn
    body = functools.partial(_outln_body, nj=nj, tn=tn, alpha=alpha, n_exp=n_exp)
    row = lambda i, j: (i, 0)
    const = lambda i, j: (0, 0)
    return pl.pallas_call(
        body,
        grid=(m // tm, nj),
        in_specs=[pl.BlockSpec((tm, d), row), pl.BlockSpec((d, tn), lambda i, j: (0, j)),
                  pl.BlockSpec((tm, d), row), pl.BlockSpec((1, d), const), pl.BlockSpec((1, d), const),
                  pl.BlockSpec((d, n_exp), const), pl.BlockSpec((1, n_exp), const)],
        out_specs=[pl.BlockSpec((tm, d), row), pl.BlockSpec((tm, 128), row), pl.BlockSpec((tm, 128), row)],
        out_shape=[jax.ShapeDtypeStruct((m, d), F32), jax.ShapeDtypeStruct((m, 128), jnp.int32),
                   jax.ShapeDtypeStruct((m, 128), F32)],
        scratch_shapes=[pltpu.VMEM((nj, tm, tn), F32)],
        compiler_params=_params(2, 48),
        name="outproj_ln_router",
    )(merged, w_out, x, ln_g, ln_b, w_router, b_router)


def _route_tables(top_idx, n_exp, tile):
    m, k = top_idx.shape
    e = top_idx.reshape(-1)
    onehot = (e[:, None] == jnp.arange(n_exp, dtype=jnp.int32)[None, :]).astype(jnp.int32)
    csum = jnp.cumsum(onehot, axis=0)
    rank = jnp.sum(csum * onehot, axis=1) - 1
    counts = csum[-1]
    ntile = (counts + tile - 1) // tile
    tend = jnp.cumsum(ntile)
    tstart = tend - ntile
    pos = jnp.sum(onehot * (tstart * tile)[None, :], axis=1) + rank
    n_used = tend[-1]
    n_tiles = (m * k) // tile + n_exp
    tix = jnp.arange(n_tiles, dtype=jnp.int32)
    te = jnp.minimum(jnp.sum((tix[:, None] >= tend[None, :]).astype(jnp.int32), axis=1), n_exp - 1)
    te = jnp.where(tix < n_used, te, te[jnp.maximum(n_used - 1, 0)])
    first = ((tix == tstart[te]) & (tix < n_used)).astype(jnp.int32)
    return pos.astype(jnp.int32), te.astype(jnp.int32), first, n_used.reshape(1).astype(jnp.int32), n_tiles


def _rowtok_body(pos_ref, rt_ref, *, n_pairs, n_rows):
    def clear(i, c):
        rt_ref[i] = 0
        return c

    lax.fori_loop(0, n_rows, clear, 0)

    def put(i, c):
        rt_ref[pos_ref[i]] = lax.div(i, TOP_K)
        return c

    lax.fori_loop(0, n_pairs, put, 0)


def _row_tokens(pos, n_rows):
    body = functools.partial(_rowtok_body, n_pairs=pos.shape[0], n_rows=n_rows)
    return pl.pallas_call(
        body,
        in_specs=[pl.BlockSpec(memory_space=pltpu.SMEM)],
        out_specs=pl.BlockSpec(memory_space=pltpu.SMEM),
        out_shape=jax.ShapeDtypeStruct((n_rows,), jnp.int32),
        name="row_tokens",
    )(pos)


def _row_copy(src_hbm, src_row, dst, dst_row, sem):
    return pltpu.make_async_copy(src_hbm.at[pl.ds(src_row, 1), :], dst.at[pl.ds(dst_row, 1), :], sem)


def _gather_body(rt_ref, nu_ref, x_hbm, o_ref, buf, sem, *, tile):
    i = pl.program_id(0)

    @pl.when(i < nu_ref[0])
    def _():
        base = i * tile

        def issue(r, c):
            _row_copy(x_hbm, rt_ref[base + r], buf, r, sem).start()
            return c

        lax.fori_loop(0, tile, issue, 0)

        def wait(r, c):
            _row_copy(x_hbm, 0, buf, r, sem).wait()
            return c

        lax.fori_loop(0, tile, wait, 0)
        o_ref[...] = buf[...].astype(o_ref.dtype)

    @pl.when(i >= nu_ref[0])
    def _():
        o_ref[...] = jnp.zeros(o_ref.shape, o_ref.dtype)


def _dispatch(x1, row_tok, n_used, n_tiles, tile):
    m, d = x1.shape
    body = functools.partial(_gather_body, tile=tile)
    grid_spec = pltpu.PrefetchScalarGridSpec(
        num_scalar_prefetch=2,
        grid=(n_tiles,),
        in_specs=[pl.BlockSpec(memory_space=pl.ANY)],
        out_specs=pl.BlockSpec((tile, d), lambda i, rt, nu: (i, 0)),
        scratch_shapes=[pltpu.VMEM((tile, d), F32), pltpu.SemaphoreType.DMA(())],
    )
    return pl.pallas_call(
        body,
        grid_spec=grid_spec,
        out_shape=jax.ShapeDtypeStruct((n_tiles * tile, d), BF16),
        compiler_params=_params(1, 32),
        name="moe_dispatch",
    )(row_tok, n_used, x1)


def _gemm1_body(te_ref, first_ref, nu_ref, x_ref, wg_ref, wu_ref, bg_ref, bu_ref, o_ref, wg_b, wu_b):
    i = pl.program_id(1)

    @pl.when(i < nu_ref[0])
    def _():
        @pl.when(first_ref[i] == 1)
        def _():
            wg_b[...] = wg_ref[...].astype(BF16)
            wu_b[...] = wu_ref[...].astype(BF16)

        x = x_ref[...]
        gate = _dot(x, wg_b[...]) + bg_ref[...]
        up = _dot(x, wu_b[...]) + bu_ref[...]
        gate = jnp.minimum(gate, SWIGLU_LIMIT)
        up = jnp.clip(up, -SWIGLU_LIMIT, SWIGLU_LIMIT)
        glu = gate * jax.nn.sigmoid(SWIGLU_ALPHA * gate)
        o_ref[...] = (glu * (up + 1.0)).astype(o_ref.dtype)

    @pl.when(i >= nu_ref[0])
    def _():
        o_ref[...] = jnp.zeros(o_ref.shape, o_ref.dtype)


def _gemm1(xs, w_in, b_in, te, first, n_used, tile):
    n_rows, d = xs.shape
    d_ff = w_in.shape[2] // 2
    tn = _pick(d_ff, (512, 256, 128))
    nj = d_ff // tn
    n_tiles = n_rows // tile
    row_blk = lambda j, i, te, fi, nu: (jnp.minimum(i, nu[0] - 1), 0)
    grid_spec = pltpu.PrefetchScalarGridSpec(
        num_scalar_prefetch=3,
        grid=(nj, n_tiles),
        in_specs=[pl.BlockSpec((tile, d), row_blk),
                  pl.BlockSpec((None, d, tn), lambda j, i, te, fi, nu: (te[i], 0, j)),
                  pl.BlockSpec((None, d, tn), lambda j, i, te, fi, nu: (te[i], 0, nj + j)),
                  pl.BlockSpec((None, 1, tn), lambda j, i, te, fi, nu: (te[i], 0, j)),
                  pl.BlockSpec((None, 1, tn), lambda j, i, te, fi, nu: (te[i], 0, nj + j))],
        out_specs=pl.BlockSpec((tile, tn), lambda j, i, te, fi, nu: (i, j)),
        scratch_shapes=[pltpu.VMEM((d, tn), BF16), pltpu.VMEM((d, tn), BF16)],
    )
    return pl.pallas_call(
        _gemm1_body,
        grid_spec=grid_spec,
        out_shape=jax.ShapeDtypeStruct((n_rows, d_ff), BF16),
        compiler_params=_params(2, 56),
        name="moe_gemm1",
    )(te, first, n_used, xs, w_in, w_in, b_in, b_in)


def _gemm2_body(te_ref, first_ref, nu_ref, a_ref, w_ref, b_ref, o_ref, w_b):
    i = pl.program_id(1)

    @pl.when(i < nu_ref[0])
    def _():
        @pl.when(first_ref[i] == 1)
        def _():
            w_b[...] = w_ref[...].astype(BF16)

        o_ref[...] = _dot(a_ref[...], w_b[...]) + b_ref[...]

    @pl.when(i >= nu_ref[0])
    def _():
        o_ref[...] = jnp.zeros(o_ref.shape, o_ref.dtype)


def _gemm2(act, w_out, b_out, te, first, n_used, tile):
    n_rows, d_ff = act.shape
    d = w_out.shape[2]
    tn = _pick(d, (512, 256, 128))
    nj = d // tn
    n_tiles = n_rows // tile
    grid_spec = pltpu.PrefetchScalarGridSpec(
        num_scalar_prefetch=3,
        grid=(nj, n_tiles),
        in_specs=[pl.BlockSpec((tile, d_ff), lambda j, i, te, fi, nu: (jnp.minimum(i, nu[0] - 1), 0)),
                  pl.BlockSpec((None, d_ff, tn), lambda j, i, te, fi, nu: (te[i], 0, j)),
                  pl.BlockSpec((None, 1, tn), lambda j, i, te, fi, nu: (te[i], 0, j))],
        out_specs=pl.BlockSpec((tile, tn), lambda j, i, te, fi, nu: (i, j)),
        scratch_shapes=[pltpu.VMEM((d_ff, tn), BF16)],
    )
    return pl.pallas_call(
        _gemm2_body,
        grid_spec=grid_spec,
        out_shape=jax.ShapeDtypeStruct((n_rows, d), F32),
        compiler_params=_params(2, 48),
        name="moe_gemm2",
    )(te, first, n_used, act, w_out, b_out)


def _combine_body(pos_ref, y_hbm, x1_ref, w_ref, g_ref, b_ref, x2_ref, x2b_ref, buf, sem, *, tile, alpha):
    i = pl.program_id(0)
    base = i * tile * TOP_K

    def issue(r, c):
        for k in range(TOP_K):
            _row_copy(y_hbm, pos_ref[base + r * TOP_K + k], buf.at[k], r, sem).start()
        return c

    lax.fori_loop(0, tile, issue, 0)

    def wait(r, c):
        for k in range(TOP_K):
            _row_copy(y_hbm, 0, buf.at[k], r, sem).wait()
        return c

    lax.fori_loop(0, tile, wait, 0)
    moe = w_ref[:, 0:1] * buf[0]
    for k in range(1, TOP_K):
        moe = moe + w_ref[:, k:k + 1] * buf[k]
    out = _layer_norm_rows(alpha * x1_ref[...] + moe, g_ref[...], b_ref[...])
    x2_ref[...] = out
    x2b_ref[...] = out.astype(BF16)


def _combine_ln(y, pos, x1, top_w, ln_g, ln_b, *, alpha, tile):
    m, d = x1.shape
    body = functools.partial(_combine_body, tile=tile, alpha=alpha)
    row = lambda i, pos: (i, 0)
    const = lambda i, pos: (0, 0)
    grid_spec = pltpu.PrefetchScalarGridSpec(
        num_scalar_prefetch=1,
        grid=(m // tile,),
        in_specs=[pl.BlockSpec(memory_space=pl.ANY), pl.BlockSpec((tile, d), row),
                  pl.BlockSpec((tile, 128), row), pl.BlockSpec((1, d), const), pl.BlockSpec((1, d), const)],
        out_specs=[pl.BlockSpec((tile, d), row), pl.BlockSpec((tile, d), row)],
        scratch_shapes=[pltpu.VMEM((TOP_K, tile, d), F32), pltpu.SemaphoreType.DMA(())],
    )
    return pl.pallas_call(
        body,
        grid_spec=grid_spec,
        out_shape=[jax.ShapeDtypeStruct((m, d), F32), jax.ShapeDtypeStruct((m, d), BF16)],
        compiler_params=_params(1, 52),
        name="moe_combine_ln",
    )(pos, y, x1, top_w, ln_g, ln_b)


def kernel(x_prompt, x_sample, mem_prompt, cache_band_k, cache_band_v, state_pool, cache_mem_k, cache_mem_v, w_in, rel_bias, w_attn_proj, w_pool_map, pool_scale, w_mem_kv, w_mem_proj, w_gate, b_gate, w_out, ln_mix_g, ln_mix_b, w_router, b_router, w_exp_in, b_exp_in, w_exp_out, b_exp_out, ln_ffn_g, ln_ffn_b):
    batch, seq, d = x_prompt.shape
    dbatch, dseq, _ = x_sample.shape
    depth = w_in.shape[0]
    n_mem = mem_prompt.shape[1]
    d_attn = w_attn_proj.shape[1]
    d_mem = w_mem_proj.shape[1]
    d_pool = w_pool_map.shape[1] * w_pool_map.shape[2]
    n_heads = d_attn // HEAD_DIM
    n_exp = w_router.shape[2]
    n_cached = cache_band_k.shape[2]
    mp, ms = batch * seq, dbatch * dseq
    alpha = (2 * depth) ** 0.25
    heads_per_step = _pick(n_heads, (4, 2, 1))
    qrows = ROW_TILE
    nkb = BAND_PREV * CHUNK // qrows + 1
    u_col_blk = (3 * d_attn) // d_pool
    qm_col_blk = (3 * d_attn + d_pool) // d_mem
    assert seq % qrows == 0 and mp % dseq == 0 and (mp + ms) % ROW_TILE == 0
    assert (3 * d_attn) % d_pool == 0 and (3 * d_attn + d_pool) % d_mem == 0

    x = jnp.concatenate([x_prompt.reshape(mp, d), x_sample.reshape(ms, d)], axis=0)
    xb = x.astype(BF16)
    mem_b = mem_prompt.reshape(batch * n_mem, d).astype(BF16)

    outs = {k: [] for k in ("bkp", "bvp", "pp", "mkp", "mvp", "bks", "bvs", "ps")}
    for l in range(depth):
        h = _matmul(xb, w_in[l].astype(BF16), F32, "in_proj")
        mem_kv = _matmul(mem_b, w_mem_kv[l].astype(BF16), F32, "mem_kv")

        bias_p = _band_bias_prompt(rel_bias[l], qrows, nkb)
        bias_c, bias_n = _band_bias_sample(rel_bias[l], dseq, n_cached)
        attn_p = _band_prompt(h, bias_p, batch=batch, seq=seq, d_attn=d_attn, heads_per_step=heads_per_step)
        attn_s = _band_sample(h, cache_band_k[l].reshape(dbatch, n_cached, d_attn),
                              cache_band_v[l].reshape(dbatch, n_cached, d_attn), bias_c, bias_n,
                              row0=mp, batch=dbatch, seq=dseq, d_attn=d_attn, heads_per_step=heads_per_step)
        attn = jnp.concatenate([attn_p, attn_s], axis=0)

        mem_p = _mem_attn(h, mem_kv, mem_kv, row0=0, batch=batch, seq=seq, d_mem=d_mem,
                          q_col_blk=qm_col_blk, k_col_blk=0, v_col_blk=1, n_mem=n_mem, name="mem_prompt")
        mem_s = _mem_attn(h, cache_mem_k[l].reshape(dbatch * n_mem, d_mem),
                          cache_mem_v[l].reshape(dbatch * n_mem, d_mem), row0=mp, batch=dbatch, seq=dseq,
                          d_mem=d_mem, q_col_blk=qm_col_blk, k_col_blk=0, v_col_blk=0, n_mem=n_mem,
                          name="mem_sample")
        mem = jnp.concatenate([mem_p, mem_s], axis=0)

        hist = jnp.pad(state_pool[l], ((0, 0), (HIST_ROWS - POOL_HIST, 0), (0, 0))).reshape(
            dbatch * HIST_ROWS, d_pool)
        mixed_p = _pool(h, None, row0=0, batch=batch, seq=seq, d_pool=d_pool, u_col_blk=u_col_blk, pos0=0,
                        name="pool_prompt")
        mixed_s = _pool(h, hist, row0=mp, batch=dbatch, seq=dseq, d_pool=d_pool, u_col_blk=u_col_blk,
                        pos0=PAST_LEN, name="pool_sample")
        mixed = jnp.concatenate([mixed_p, mixed_s], axis=0)

        merged = _merge(xb, w_gate[l].astype(BF16), b_gate[l].reshape(1, -1), attn,
                        w_attn_proj[l].astype(BF16), mixed,
                        w_pool_map[l].reshape(d_pool, -1).astype(BF16), pool_scale[l].reshape(1, -1),
                        mem, w_mem_proj[l].astype(BF16))
        x1, top_idx, top_w = _outln_router(
            merged, w_out[l].astype(BF16), x, ln_mix_g[l].reshape(1, -1), ln_mix_b[l].reshape(1, -1),
            w_router[l], b_router[l].reshape(1, -1), alpha=alpha)

        pos, te, first, n_used, n_tiles = _route_tables(top_idx[:, :TOP_K], n_exp, ROW_TILE)
        row_tok = _row_tokens(pos, n_tiles * ROW_TILE)
        xs = _dispatch(x1, row_tok, n_used, n_tiles, ROW_TILE)
        act = _gemm1(xs, w_exp_in[l], b_exp_in[l].reshape(n_exp, 1, -1), te, first, n_used, ROW_TILE)
        y = _gemm2(act, w_exp_out[l], b_exp_out[l].reshape(n_exp, 1, -1), te, first, n_used, ROW_TILE)
        x, xb = _combine_ln(y, pos, x1, top_w, ln_ffn_g[l].reshape(1, -1), ln_ffn_b[l].reshape(1, -1),
                            alpha=alpha, tile=ROW_TILE)

        hp = h[:mp].reshape(batch, seq, -1)
        hs = h[mp:].reshape(dbatch, dseq, -1)
        keep = min(BAND_PREV * CHUNK, seq)
        outs["bkp"].append(hp[:, seq - keep:, d_attn:2 * d_attn].reshape(batch, keep, n_heads, HEAD_DIM))
        outs["bvp"].append(hp[:, seq - keep:, 2 * d_attn:3 * d_attn].reshape(batch, keep, n_heads, HEAD_DIM))
        u_p = hp[:, :, 3 * d_attn:3 * d_attn + d_pool]
        outs["pp"].append(jnp.concatenate(
            [jnp.zeros((batch, POOL_HIST, d_pool), F32), u_p], axis=1)[:, -POOL_HIST:])
        hd_mem = d_mem // MEM_HEADS
        outs["mkp"].append(mem_kv[:, :d_mem].reshape(batch, n_mem, MEM_HEADS, hd_mem))
        outs["mvp"].append(mem_kv[:, d_mem:].reshape(batch, n_mem, MEM_HEADS, hd_mem))
        outs["bks"].append(hs[:, :, d_attn:2 * d_attn].reshape(dbatch, dseq, n_heads, HEAD_DIM))
        outs["bvs"].append(hs[:, :, 2 * d_attn:3 * d_attn].reshape(dbatch, dseq, n_heads, HEAD_DIM))
        u_s = hs[:, :, 3 * d_attn:3 * d_attn + d_pool]
        outs["ps"].append(jnp.concatenate([state_pool[l], u_s], axis=1)[:, -POOL_HIST:])

    st = lambda k: jnp.stack(outs[k])
    return (x[:mp].reshape(batch, seq, d), x[mp:].reshape(dbatch, dseq, d),
            st("bkp"), st("bvp"), st("pp"), st("mkp"), st("mvp"), st("bks"), st("bvs"), st("ps"))
```

```python
import functools

import numpy as np
import jax
import jax.numpy as jnp
from jax import lax
from jax.experimental import pallas as pl
from jax.experimental.pallas import tpu as pltpu

CHUNK = 64
BAND_PREV = 8
HEAD_DIM = 128
POOL_WINDOWS = (2, 4, 8, 16)
POOL_HIST = max(POOL_WINDOWS) - 1
HIST_ROWS = POOL_HIST + 1
MEM_HEADS = 4
N_BRANCHES = 3
TOP_K = 4
SWIGLU_LIMIT = 7.0
SWIGLU_ALPHA = 1.702
LN_EPS = 1e-5
NEG_INF = -1e30
PAST_LEN = 4096

ROW_TILE = 256
SUPER_TILES = 6
MIB = 1024 * 1024

F32 = jnp.float32
BF16 = jnp.bfloat16


def _pick(n, cands):
    for c in cands:
        if n % c == 0:
            return c
    return n


def _params(n_axes, vmem_mib):
    return pltpu.CompilerParams(
        dimension_semantics=("arbitrary",) * n_axes, vmem_limit_bytes=int(vmem_mib * MIB))


def _dot(a, b):
    return jnp.dot(a, b, preferred_element_type=F32)


def _dot_t(a, b):
    return lax.dot_general(a, b, (((1,), (1,)), ((), ())), preferred_element_type=F32)


def _mm_body(a_ref, w_ref, o_ref):
    o_ref[...] = _dot(a_ref[...], w_ref[...]).astype(o_ref.dtype)


def _matmul(a, w, out_dtype, name):
    m, k = a.shape
    n = w.shape[1]
    tm = _pick(m, (1088, 1024, 512, 256, 128))
    tn = _pick(n, (1024, 512, 256, 128))
    return pl.pallas_call(
        _mm_body,
        grid=(m // tm, n // tn),
        in_specs=[pl.BlockSpec((tm, k), lambda i, j: (i, 0)),
                  pl.BlockSpec((k, tn), lambda i, j: (0, j))],
        out_specs=pl.BlockSpec((tm, tn), lambda i, j: (i, j)),
        out_shape=jax.ShapeDtypeStruct((m, n), out_dtype),
        compiler_params=_params(2, 56),
        name=name,
    )(a, w)


def _softmax_pv(scores, values):
    m = scores[0].max(axis=-1, keepdims=True)
    for s in scores[1:]:
        m = jnp.maximum(m, s.max(axis=-1, keepdims=True))
    es = [jnp.exp(s - m) for s in scores]
    l = es[0].sum(axis=-1, keepdims=True)
    for e in es[1:]:
        l = l + e.sum(axis=-1, keepdims=True)
    inv = 1.0 / l
    o = None
    for e, v in zip(es, values):
        oj = _dot((e * inv).astype(BF16), v)
        o = oj if o is None else o + oj
    return o


def _band_prompt_body(*refs, heads, nkb, qrows, scale):
    q_ref = refs[0]
    k_refs = refs[1:1 + nkb]
    v_refs = refs[1 + nkb:1 + 2 * nkb]
    bias_ref = refs[1 + 2 * nkb]
    o_ref = refs[2 + 2 * nkb]
    qb = pl.program_id(2)
    for h in range(heads):
        sl = slice(h * HEAD_DIM, (h + 1) * HEAD_DIM)
        q = q_ref[:, sl].astype(BF16)
        scores = []
        for j in range(nkb):
            s = _dot_t(q, k_refs[j][:, sl].astype(BF16)) * scale
            s = s + bias_ref[h, :, j * qrows:(j + 1) * qrows]
            if j < nkb - 1:
                s = jnp.where(qb >= nkb - 1 - j, s, NEG_INF)
            scores.append(s)
        o = _softmax_pv(scores, [v_refs[j][:, sl].astype(BF16) for j in range(nkb)])
        o_ref[:, sl] = o.astype(o_ref.dtype)


def _band_prompt(h, bias, *, batch, seq, d_attn, heads_per_step):
    qrows = ROW_TILE
    nkb = BAND_PREV * CHUNK // qrows + 1
    nq = seq // qrows
    gw = heads_per_step * HEAD_DIM
    hg = d_attn // gw
    q_spec = pl.BlockSpec((qrows, gw), lambda g, b, t: (b * nq + t, g))

    def kv_spec(j, col0):
        back = nkb - 1 - j
        return pl.BlockSpec((qrows, gw),
                            lambda g, b, t: (b * nq + jnp.maximum(t - back, 0), col0 + g))

    in_specs = ([q_spec] + [kv_spec(j, hg) for j in range(nkb)]
                + [kv_spec(j, 2 * hg) for j in range(nkb)]
                + [pl.BlockSpec((heads_per_step, qrows, nkb * qrows), lambda g, b, t: (g, 0, 0))])
    body = functools.partial(_band_prompt_body, heads=heads_per_step, nkb=nkb, qrows=qrows,
                             scale=HEAD_DIM ** -0.5)
    return pl.pallas_call(
        body,
        grid=(hg, batch, nq),
        in_specs=in_specs,
        out_specs=pl.BlockSpec((qrows, gw), lambda g, b, t: (b * nq + t, g)),
        out_shape=jax.ShapeDtypeStruct((batch * seq, d_attn), BF16),
        compiler_params=_params(3, 40),
        name="band_prompt",
    )(*([h] * (1 + 2 * nkb)), bias)


def _band_sample_body(q_ref, kc_ref, kn_ref, vc_ref, vn_ref, bc_ref, bn_ref, o_ref, *, heads, scale):
    for h in range(heads):
        sl = slice(h * HEAD_DIM, (h + 1) * HEAD_DIM)
        q = q_ref[:, sl].astype(BF16)
        sc = _dot_t(q, kc_ref[:, sl].astype(BF16)) * scale + bc_ref[h]
        sn = _dot_t(q, kn_ref[:, sl].astype(BF16)) * scale + bn_ref[h]
        o = _softmax_pv([sc, sn], [vc_ref[:, sl].astype(BF16), vn_ref[:, sl].astype(BF16)])
        o_ref[:, sl] = o.astype(o_ref.dtype)


def _band_sample(h, cache_k, cache_v, bias_c, bias_n, *, row0, batch, seq, d_attn, heads_per_step):
    n_cached = cache_k.shape[1]
    gw = heads_per_step * HEAD_DIM
    hg = d_attn // gw
    rb0 = row0 // seq
    new_spec = lambda col0: pl.BlockSpec((seq, gw), lambda g, b: (rb0 + b, col0 + g))
    cache_spec = pl.BlockSpec((None, n_cached, gw), lambda g, b: (b, 0, g))
    body = functools.partial(_band_sample_body, heads=heads_per_step, scale=HEAD_DIM ** -0.5)
    return pl.pallas_call(
        body,
        grid=(hg, batch),
        in_specs=[new_spec(0), cache_spec, new_spec(hg), cache_spec, new_spec(2 * hg),
                  pl.BlockSpec((heads_per_step, seq, n_cached), lambda g, b: (g, 0, 0)),
                  pl.BlockSpec((heads_per_step, seq, seq), lambda g, b: (g, 0, 0))],
        out_specs=pl.BlockSpec((seq, gw), lambda g, b: (b, g)),
        out_shape=jax.ShapeDtypeStruct((batch * seq, d_attn), BF16),
        compiler_params=_params(2, 40),
        name="band_sample",
    )(h, cache_k, h, cache_v, h, bias_c, bias_n)


def _toeplitz_bias(rel_bias, rows, width, off):
    max_rel = (rel_bias.shape[1] - 1) // 2
    n = 1 << int(np.ceil(np.log2(width + rows)))
    u = np.arange(n)
    lag = np.where(u < width, u, u - n)
    rel = np.clip(off - lag, -max_rel, max_rel) + max_rel
    row = rel_bias[:, jnp.asarray(rel)].astype(F32)
    skew = jnp.tile(row, (1, rows))[:, :rows * (n - 1)].reshape(row.shape[0], rows, n - 1)
    return skew[:, :, :width]


def _band_bias_prompt(rel_bias, qrows, nkb):
    i = np.arange(qrows)[:, None]
    j = np.arange(nkb * qrows)[None, :]
    cq, ck = i // CHUNK, j // CHUNK
    in_band = (ck >= cq) & (ck <= cq + BAND_PREV)
    bias = _toeplitz_bias(rel_bias, qrows, nkb * qrows, (nkb - 1) * qrows)
    return jnp.where(jnp.asarray(in_band)[None], bias, NEG_INF)


def _band_bias_sample(rel_bias, seq, n_cached):
    bias = _toeplitz_bias(rel_bias, seq, n_cached + seq, n_cached)
    return bias[:, :, :n_cached], bias[:, :, n_cached:]


def _mem_body(q_ref, k_ref, v_ref, o_ref, *, heads, hd, scale):
    for h in range(heads):
        sl = slice(h * hd, (h + 1) * hd)
        s = _dot_t(q_ref[:, sl].astype(BF16), k_ref[:, sl].astype(BF16)) * scale
        o = _softmax_pv([s], [v_ref[:, sl].astype(BF16)])
        o_ref[:, sl] = o.astype(o_ref.dtype)


def _mem_attn(h, kmat, vmat, *, row0, batch, seq, d_mem, q_col_blk, k_col_blk, v_col_blk, n_mem, name):
    tq = _pick(seq, (512, 256, 128, 64, 32))
    nt = seq // tq
    rb0 = row0 // tq
    hd = d_mem // MEM_HEADS
    body = functools.partial(_mem_body, heads=MEM_HEADS, hd=hd, scale=hd ** -0.5)
    return pl.pallas_call(
        body,
        grid=(batch, nt),
        in_specs=[pl.BlockSpec((tq, d_mem), lambda b, t: (rb0 + b * nt + t, q_col_blk)),
                  pl.BlockSpec((n_mem, d_mem), lambda b, t: (b, k_col_blk)),
                  pl.BlockSpec((n_mem, d_mem), lambda b, t: (b, v_col_blk))],
        out_specs=pl.BlockSpec((tq, d_mem), lambda b, t: (b * nt + t, 0)),
        out_shape=jax.ShapeDtypeStruct((batch * seq, d_mem), BF16),
        compiler_params=_params(2, 40),
        name=name,
    )(h, kmat, vmat)


def _pool_body(u_ref, prev_ref, o_ref, ext_ref, *, rows, gc, pos0, zero_first_hist):
    t = pl.program_id(1)
    prev = prev_ref[...]
    if zero_first_hist:
        prev = jnp.where(t == 0, 0.0, prev)
    ext_ref[0:HIST_ROWS, :] = prev
    ext_ref[HIST_ROWS:HIST_ROWS + rows, :] = u_ref[...]
    pos = pos0 + t * rows + lax.broadcasted_iota(jnp.int32, (rows, 1), 0)
    for g, w in enumerate(POOL_WINDOWS):
        cs = slice(g * gc, (g + 1) * gc)
        acc = ext_ref[HIST_ROWS:HIST_ROWS + rows, cs]
        for back in range(1, w):
            acc = acc + ext_ref[HIST_ROWS - back:HIST_ROWS - back + rows, cs]
        cnt = jnp.minimum(w, pos + 1).astype(F32)
        o_ref[:, cs] = (acc / cnt - u_ref[:, cs]).astype(o_ref.dtype)


def _pool(h, hist, *, row0, batch, seq, d_pool, u_col_blk, pos0, name):
    rows = _pick(seq, (ROW_TILE, 128, 64, 32))
    nt = seq // rows
    rb0 = row0 // rows
    gc = d_pool // len(POOL_WINDOWS)
    if hist is None:
        per = rows // HIST_ROWS
        prev_arr = h
        prev_spec = pl.BlockSpec(
            (HIST_ROWS, d_pool),
            lambda b, t: (jnp.maximum((rb0 + b * nt + t) * per - 1, 0), u_col_blk))
    else:
        assert nt == 1
        prev_arr = hist
        prev_spec = pl.BlockSpec((HIST_ROWS, d_pool), lambda b, t: (b, 0))
    body = functools.partial(_pool_body, rows=rows, gc=gc, pos0=pos0, zero_first_hist=hist is None)
    return pl.pallas_call(
        body,
        grid=(batch, nt),
        in_specs=[pl.BlockSpec((rows, d_pool), lambda b, t: (rb0 + b * nt + t, u_col_blk)), prev_spec],
        out_specs=pl.BlockSpec((rows, d_pool), lambda b, t: (b * nt + t, 0)),
        out_shape=jax.ShapeDtypeStruct((batch * seq, d_pool), BF16),
        scratch_shapes=[pltpu.VMEM((HIST_ROWS + rows, d_pool), F32)],
        compiler_params=_params(2, 32),
        name=name,
    )(h, prev_arr)


def _merge_body(x_ref, wg0, wg1, wg2, bg0, bg1, bg2, a_ref, wa_ref, p_ref, wp_ref, ps_ref,
                m_ref, wm_ref, o_ref):
    x = x_ref[...]

    def gate(w_ref, b_ref):
        return jax.nn.sigmoid(_dot(x, w_ref[...]) + b_ref[...])

    br_attn = _dot(a_ref[...], wa_ref[...])
    br_pool = _dot(p_ref[...], wp_ref[...]) * ps_ref[...]
    br_mem = _dot(m_ref[...], wm_ref[...])
    merged = gate(wg0, bg0) * br_attn + gate(wg1, bg1) * br_pool + gate(wg2, bg2) * br_mem
    o_ref[...] = merged.astype(o_ref.dtype)


def _merge(xb, w_gate, b_gate, attn, w_attn, mixed, w_pool, pool_scale, mem, w_mem):
    m, d = xb.shape
    d_attn, d_pool, d_mem = attn.shape[1], mixed.shape[1], mem.shape[1]
    gc = d_pool // len(POOL_WINDOWS)
    pool_out = w_pool.shape[1]
    tm = _pick(m, (512, 256, 128))
    tn = _pick(pool_out, (256, 128))
    nj = d // tn
    per_group = pool_out // tn
    row = lambda i, j: (i, 0)
    gate_w = lambda br: pl.BlockSpec((d, tn), lambda i, j: (0, br * nj + j))
    gate_b = lambda br: pl.BlockSpec((1, tn), lambda i, j: (0, br * nj + j))
    return pl.pallas_call(
        _merge_body,
        grid=(m // tm, nj),
        in_specs=[pl.BlockSpec((tm, d), row), gate_w(0), gate_w(1), gate_w(2),
                  gate_b(0), gate_b(1), gate_b(2),
                  pl.BlockSpec((tm, d_attn), row), pl.BlockSpec((d_attn, tn), lambda i, j: (0, j)),
                  pl.BlockSpec((tm, gc), lambda i, j: (i, j // per_group)),
                  pl.BlockSpec((gc, tn), lambda i, j: (j // per_group, j % per_group)),
                  pl.BlockSpec((1, tn), lambda i, j: (0, j)),
                  pl.BlockSpec((tm, d_mem), row), pl.BlockSpec((d_mem, tn), lambda i, j: (0, j))],
        out_specs=pl.BlockSpec((tm, tn), lambda i, j: (i, j)),
        out_shape=jax.ShapeDtypeStruct((m, d), BF16),
        compiler_params=_params(2, 48),
        name="merge",
    )(xb, w_gate, w_gate, w_gate, b_gate, b_gate, b_gate, attn, w_attn, mixed, w_pool, pool_scale,
      mem, w_mem)


def _layer_norm_rows(y, g, b):
    mu = jnp.mean(y, axis=-1, keepdims=True)
    yc = y - mu
    var = jnp.mean(yc * yc, axis=-1, keepdims=True)
    return yc * lax.rsqrt(var + LN_EPS) * g + b


def _outproj_body(a_ref, w_ref, x_ref, o_ref, *, alpha):
    o_ref[...] = alpha * x_ref[...] + _dot(a_ref[...], w_ref[...])


def _outproj_residual(merged, w_out, x, *, alpha):
    m, d = x.shape
    tm = _pick(m, (544, 512, 256, 128))
    tn = _pick(d, (1024, 512, 256, 128))
    return pl.pallas_call(
        functools.partial(_outproj_body, alpha=alpha),
        grid=(m // tm, d // tn),
        in_specs=[pl.BlockSpec((tm, d), lambda i, j: (i, 0)), pl.BlockSpec((d, tn), lambda i, j: (0, j)),
                  pl.BlockSpec((tm, tn), lambda i, j: (i, j))],
        out_specs=pl.BlockSpec((tm, tn), lambda i, j: (i, j)),
        out_shape=jax.ShapeDtypeStruct((m, d), F32),
        compiler_params=_params(2, 48),
        name="out_proj",
    )(merged, w_out, x)


def _ln_router_body(y_ref, g_ref, b_ref, wr_ref, br_ref, x1_ref, idx_ref, wgt_ref, *, n_exp):
    tm = y_ref.shape[0]
    xn = _layer_norm_rows(y_ref[...], g_ref[...], b_ref[...])
    x1_ref[...] = xn
    logits = _dot(xn.astype(BF16), wr_ref[...].astype(BF16)) + br_ref[...]

    col = lax.broadcasted_iota(jnp.int32, (tm, n_exp), 1).astype(F32)
    vals, idxs = [], []
    for _ in range(TOP_K):
        mx = logits.max(axis=-1, keepdims=True)
        ix = jnp.where(logits == mx, col, float(n_exp)).min(axis=-1, keepdims=True)
        vals.append(mx)
        idxs.append(ix)
        logits = jnp.where(col == ix, -jnp.inf, logits)
    es = [jnp.exp(v - vals[0]) for v in vals]
    den = es[0]
    for e in es[1:]:
        den = den + e
    lane = lax.broadcasted_iota(jnp.int32, idx_ref.shape, 1)
    idx_out = jnp.zeros(idx_ref.shape, F32)
    wgt_out = jnp.zeros(wgt_ref.shape, F32)
    for k in range(TOP_K):
        idx_out = jnp.where(lane == k, idxs[k], idx_out)
        wgt_out = jnp.where(lane == k, es[k] / den, wgt_out)
    idx_ref[...] = idx_out.astype(jnp.int32)
    wgt_ref[...] = wgt_out


def _ln_router(y, ln_g, ln_b, w_router, b_router):
    m, d = y.shape
    n_exp = w_router.shape[1]
    tm = ROW_TILE
    row = lambda i: (i, 0)
    const = lambda i: (0, 0)
    return pl.pallas_call(
        functools.partial(_ln_router_body, n_exp=n_exp),
        grid=(m // tm,),
        in_specs=[pl.BlockSpec((tm, d), row), pl.BlockSpec((1, d), const), pl.BlockSpec((1, d), const),
                  pl.BlockSpec((d, n_exp), const), pl.BlockSpec((1, n_exp), const)],
        out_specs=[pl.BlockSpec((tm, d), row), pl.BlockSpec((tm, 128), row), pl.BlockSpec((tm, 128), row)],
        out_shape=[jax.ShapeDtypeStruct((m, d), F32), jax.ShapeDtypeStruct((m, 128), jnp.int32),
                   jax.ShapeDtypeStruct((m, 128), F32)],
        compiler_params=_params(1, 40),
        name="ln_router",
    )(y, ln_g, ln_b, w_router, b_router)


def _route_tables(top_idx, n_exp):
    m, k = top_idx.shape
    i32 = jnp.int32
    e = top_idx.reshape(-1)
    onehot = (e[:, None] == jnp.arange(n_exp, dtype=i32)[None, :]).astype(i32)
    csum = jnp.cumsum(onehot, axis=0)
    rank = jnp.sum(csum * onehot, axis=1) - 1
    counts = csum[-1]
    ntile = (counts + ROW_TILE - 1) // ROW_TILE
    tend = jnp.cumsum(ntile)
    tstart = tend - ntile
    pos = jnp.sum(onehot * (tstart * ROW_TILE)[None, :], axis=1) + rank
    n_tiles = (m * k) // ROW_TILE + n_exp
    n_super = (m * k) // (ROW_TILE * SUPER_TILES) + n_exp

    nsup = (ntile + SUPER_TILES - 1) // SUPER_TILES
    send = jnp.cumsum(nsup)
    sstart = send - nsup
    six = jnp.arange(n_super, dtype=i32)
    se = jnp.minimum(jnp.sum((six[:, None] >= send[None, :]).astype(i32), axis=1), n_exp - 1)
    live = six < send[-1]
    se = jnp.where(live, se, se[jnp.maximum(send[-1] - 1, 0)])
    sub = (six - sstart[se]) * SUPER_TILES
    tile0 = jnp.where(live, tstart[se] + sub, 0)
    ntile_s = jnp.where(live, jnp.minimum(ntile[se] - sub, SUPER_TILES), 0)
    tables = dict(
        pos=pos.astype(i32), se=se.astype(i32), tile0=tile0.astype(i32), ntile_s=ntile_s.astype(i32),
        n_super_used=send[-1].reshape(1).astype(i32), n_tiles_used=tend[-1].reshape(1).astype(i32),
        valid_end=(tstart * ROW_TILE + counts).astype(i32), pad_end=(tend * ROW_TILE).astype(i32))
    return tables, n_tiles, n_super


def _rowtok_body(pos_ref, vend_ref, pend_ref, ntu_ref, rt_ref, *, n_pairs, n_rows, n_exp):
    def clear(i, c):
        rt_ref[i] = 0
        return c

    def clear_pad(e, c):
        return lax.fori_loop(vend_ref[e], pend_ref[e], clear, c)

    lax.fori_loop(0, n_exp, clear_pad, 0)
    lax.fori_loop(ntu_ref[0] * ROW_TILE, n_rows, clear, 0)

    def put(i, c):
        rt_ref[pos_ref[i]] = lax.div(i, TOP_K)
        return c

    lax.fori_loop(0, n_pairs, put, 0, unroll=8)


def _row_tokens(tables, n_rows, n_exp):
    pos = tables["pos"]
    body = functools.partial(_rowtok_body, n_pairs=pos.shape[0], n_rows=n_rows, n_exp=n_exp)
    smem = pl.BlockSpec(memory_space=pltpu.SMEM)
    return pl.pallas_call(
        body,
        in_specs=[smem, smem, smem, smem],
        out_specs=smem,
        out_shape=jax.ShapeDtypeStruct((n_rows,), jnp.int32),
        name="row_tokens",
    )(pos, tables["valid_end"], tables["pad_end"], tables["n_tiles_used"])


def _row_copy(src_hbm, src_row, dst, dst_row, sem):
    return pltpu.make_async_copy(src_hbm.at[pl.ds(src_row, 1), :], dst.at[pl.ds(dst_row, 1), :], sem)


def _gather_body(rt_ref, nu_ref, x_hbm, o_ref, buf, sem, *, tile):
    i = pl.program_id(0)
    nu = nu_ref[0]
    slot = lax.rem(i, 2)

    def issue(t, sl):
        def one(r, c):
            _row_copy(x_hbm, rt_ref[t * tile + r], buf.at[sl], r, sem.at[sl]).start()
            return c

        lax.fori_loop(0, tile, one, 0, unroll=8)

    def wait(sl):
        def one(r, c):
            _row_copy(x_hbm, 0, buf.at[sl], r, sem.at[sl]).wait()
            return c

        lax.fori_loop(0, tile, one, 0, unroll=8)

    @pl.when(i == 0)
    def _():
        issue(0, 0)

    @pl.when(i < nu)
    def _():
        wait(slot)

        @pl.when(i + 1 < nu)
        def _():
            issue(i + 1, 1 - slot)

        o_ref[...] = buf[slot].astype(o_ref.dtype)

    @pl.when(i >= nu)
    def _():
        o_ref[...] = jnp.zeros(o_ref.shape, o_ref.dtype)


def _dispatch(x1, row_tok, n_used, n_tiles, tile):
    m, d = x1.shape
    body = functools.partial(_gather_body, tile=tile)
    grid_spec = pltpu.PrefetchScalarGridSpec(
        num_scalar_prefetch=2,
        grid=(n_tiles,),
        in_specs=[pl.BlockSpec(memory_space=pl.ANY)],
        out_specs=pl.BlockSpec((tile, d), lambda i, rt, nu: (i, 0)),
        scratch_shapes=[pltpu.VMEM((2, tile, d), F32), pltpu.SemaphoreType.DMA((2,))],
    )
    return pl.pallas_call(
        body,
        grid_spec=grid_spec,
        out_shape=jax.ShapeDtypeStruct((n_tiles * tile, d), BF16),
        compiler_params=_params(1, 32),
        name="moe_dispatch",
    )(row_tok, n_used, x1)


def _grouped_body(se_ref, t0_ref, nt_ref, nsu_ref, ntu_ref, x_hbm, *rest,
                  n_w, tn, nj, n_super, n_tiles, epilogue):
    w_refs, b_refs = rest[:n_w], rest[n_w:2 * n_w]
    out_hbm = rest[2 * n_w]
    xres, wb, ostage, xsem, osem, flags = rest[2 * n_w + 1:]
    s, j = pl.program_id(0), pl.program_id(1)
    nsu = nsu_ref[0]
    live = s < nsu
    slot = lax.rem(s, 2)

    def rows(t):
        return pl.ds(pl.multiple_of(t * ROW_TILE, ROW_TILE), ROW_TILE)

    def lhs_copy(tile, q, sl):
        return pltpu.make_async_copy(x_hbm.at[rows(tile), :], xres.at[sl, rows(q), :], xsem.at[sl])

    def issue_lhs(sup, sl):
        def one(q, c):
            lhs_copy(t0_ref[sup] + q, q, sl).start()
            return c

        lax.fori_loop(0, nt_ref[sup], one, 0)

    def wait_lhs(sup, sl):
        def one(q, c):
            lhs_copy(0, q, sl).wait()
            return c

        lax.fori_loop(0, nt_ref[sup], one, 0)

    def out_copy(p, tile, col0):
        return pltpu.make_async_copy(ostage.at[p], out_hbm.at[rows(tile), pl.ds(col0, tn)], osem.at[p])

    @pl.when((s == 0) & (j == 0))
    def _():
        flags[0] = 0
        flags[1] = 0
        issue_lhs(0, 0)

    @pl.when(live & (j == 0))
    def _():
        wait_lhs(s, slot)

        @pl.when(s + 1 < nsu)
        def _():
            issue_lhs(s + 1, 1 - slot)

    @pl.when(live)
    def _():
        for k, w_ref in enumerate(w_refs):
            wb[:, k * tn:(k + 1) * tn] = w_ref[...].astype(BF16)
        col0 = pl.multiple_of(j * tn, tn)

        def sub(q, c):
            res = epilogue(_dot(xres[slot, rows(q), :], wb[...]), [b[...] for b in b_refs])
            p = lax.rem(q, 2)

            @pl.when(flags[p] == 1)
            def _():
                out_copy(p, 0, 0).wait()

            ostage[p] = res.astype(ostage.dtype)
            out_copy(p, t0_ref[s] + q, col0).start()
            flags[p] = 1
            return c

        lax.fori_loop(0, nt_ref[s], sub, 0)

    @pl.when((s == n_super - 1) & (j == nj - 1))
    def _():
        for p in range(2):
            @pl.when(flags[p] == 1)
            def _():
                out_copy(p, 0, 0).wait()
                flags[p] = 0

        ostage[0] = jnp.zeros(ostage.shape[1:], ostage.dtype)

        def zero_tile(t, c):
            for jj in range(nj):
                out_copy(0, t, jj * tn).start()
            for jj in range(nj):
                out_copy(0, 0, 0).wait()
            return c

        lax.fori_loop(ntu_ref[0], n_tiles, zero_tile, 0)


def _grouped_gemm(xs, weights, biases, layer, col_blocks, tables, n_tiles, n_super, *,
                  tn, n_out, out_dtype, epilogue, name):
    n_rows, d = xs.shape
    n_w = len(weights)
    nj = n_out // tn

    def wmap(blk0):
        def index(s, j, se, t0, nt, nsu, ntu):
            return (layer, se[s], 0, blk0 + jnp.where(s < nsu[0], j, nj - 1))
        return index

    in_specs = [pl.BlockSpec(memory_space=pl.ANY)]
    in_specs += [pl.BlockSpec((None, None, d, tn), wmap(b0)) for b0 in col_blocks]
    in_specs += [pl.BlockSpec((None, None, 1, tn), wmap(b0)) for b0 in col_blocks]
    body = functools.partial(_grouped_body, n_w=n_w, tn=tn, nj=nj, n_super=n_super, n_tiles=n_tiles,
                             epilogue=epilogue)
    grid_spec = pltpu.PrefetchScalarGridSpec(
        num_scalar_prefetch=5,
        grid=(n_super, nj),
        in_specs=in_specs,
        out_specs=pl.BlockSpec(memory_space=pl.ANY),
        scratch_shapes=[pltpu.VMEM((2, SUPER_TILES * ROW_TILE, d), BF16),
                        pltpu.VMEM((d, n_w * tn), BF16),
                        pltpu.VMEM((2, ROW_TILE, tn), out_dtype),
                        pltpu.SemaphoreType.DMA((2,)), pltpu.SemaphoreType.DMA((2,)),
                        pltpu.SMEM((2,), jnp.int32)],
    )
    return pl.pallas_call(
        body,
        grid_spec=grid_spec,
        out_shape=jax.ShapeDtypeStruct((n_rows, n_out), out_dtype),
        compiler_params=_params(2, 56),
        name=name,
    )(tables["se"], tables["tile0"], tables["ntile_s"], tables["n_super_used"], tables["n_tiles_used"],
      xs, *weights, *biases)


def _swiglu_epilogue(h, biases):
    tn = h.shape[1] // 2
    gate = jnp.minimum(h[:, :tn] + biases[0], SWIGLU_LIMIT)
    up = jnp.clip(h[:, tn:] + biases[1], -SWIGLU_LIMIT, SWIGLU_LIMIT)
    glu = gate * jax.nn.sigmoid(SWIGLU_ALPHA * gate)
    return glu * (up + 1.0)


def _bias_epilogue(h, biases):
    return h + biases[0]


def _gemm1(xs, w_in, b_in, layer, tables, n_tiles, n_super):
    d_ff = w_in.shape[3] // 2
    tn = _pick(d_ff, (256, 128))
    return _grouped_gemm(xs, [w_in, w_in], [b_in, b_in], layer, [0, d_ff // tn], tables, n_tiles, n_super,
                         tn=tn, n_out=d_ff, out_dtype=BF16, epilogue=_swiglu_epilogue, name="moe_gemm1")


def _gemm2(act, w_out, b_out, layer, tables, n_tiles, n_super):
    d = w_out.shape[3]
    tn = _pick(d, (512, 256, 128))
    return _grouped_gemm(act, [w_out], [b_out], layer, [0], tables, n_tiles, n_super,
                         tn=tn, n_out=d, out_dtype=F32, epilogue=_bias_epilogue, name="moe_gemm2")


def _combine_body(pos_ref, y_hbm, x1_ref, w_ref, g_ref, b_ref, x2_ref, x2b_ref, buf, sem, *, tile, alpha):
    i = pl.program_id(0)
    base = i * tile * TOP_K

    def issue(r, c):
        for k in range(TOP_K):
            _row_copy(y_hbm, pos_ref[base + r * TOP_K + k], buf.at[k], r, sem).start()
        return c

    lax.fori_loop(0, tile, issue, 0)

    def wait(r, c):
        for k in range(TOP_K):
            _row_copy(y_hbm, 0, buf.at[k], r, sem).wait()
        return c

    lax.fori_loop(0, tile, wait, 0)
    moe = w_ref[:, 0:1] * buf[0]
    for k in range(1, TOP_K):
        moe = moe + w_ref[:, k:k + 1] * buf[k]
    out = _layer_norm_rows(alpha * x1_ref[...] + moe, g_ref[...], b_ref[...])
    x2_ref[...] = out
    x2b_ref[...] = out.astype(BF16)


def _combine_ln(y, pos, x1, top_w, ln_g, ln_b, *, alpha, tile):
    m, d = x1.shape
    body = functools.partial(_combine_body, tile=tile, alpha=alpha)
    row = lambda i, pos: (i, 0)
    const = lambda i, pos: (0, 0)
    grid_spec = pltpu.PrefetchScalarGridSpec(
        num_scalar_prefetch=1,
        grid=(m // tile,),
        in_specs=[pl.BlockSpec(memory_space=pl.ANY), pl.BlockSpec((tile, d), row),
                  pl.BlockSpec((tile, 128), row), pl.BlockSpec((1, d), const), pl.BlockSpec((1, d), const)],
        out_specs=[pl.BlockSpec((tile, d), row), pl.BlockSpec((tile, d), row)],
        scratch_shapes=[pltpu.VMEM((TOP_K, tile, d), F32), pltpu.SemaphoreType.DMA(())],
    )
    return pl.pallas_call(
        body,
        grid_spec=grid_spec,
        out_shape=[jax.ShapeDtypeStruct((m, d), F32), jax.ShapeDtypeStruct((m, d), BF16)],
        compiler_params=_params(1, 52),
        name="moe_combine_ln",
    )(pos, y, x1, top_w, ln_g, ln_b)


def kernel(x_prompt, x_sample, mem_prompt, cache_band_k, cache_band_v, state_pool, cache_mem_k, cache_mem_v, w_in, rel_bias, w_attn_proj, w_pool_map, pool_scale, w_mem_kv, w_mem_proj, w_gate, b_gate, w_out, ln_mix_g, ln_mix_b, w_router, b_router, w_exp_in, b_exp_in, w_exp_out, b_exp_out, ln_ffn_g, ln_ffn_b):
    batch, seq, d = x_prompt.shape
    dbatch, dseq, _ = x_sample.shape
    depth = w_in.shape[0]
    n_mem = mem_prompt.shape[1]
    d_attn = w_attn_proj.shape[1]
    d_mem = w_mem_proj.shape[1]
    d_pool = w_pool_map.shape[1] * w_pool_map.shape[2]
    n_heads = d_attn // HEAD_DIM
    n_exp = w_router.shape[2]
    n_cached = cache_band_k.shape[2]
    mp, ms = batch * seq, dbatch * dseq
    alpha = (2 * depth) ** 0.25
    heads_per_step = _pick(n_heads, (4, 2, 1))
    qrows = ROW_TILE
    nkb = BAND_PREV * CHUNK // qrows + 1
    u_col_blk = (3 * d_attn) // d_pool
    qm_col_blk = (3 * d_attn + d_pool) // d_mem
    assert seq % qrows == 0 and mp % dseq == 0 and (mp + ms) % ROW_TILE == 0
    assert (3 * d_attn) % d_pool == 0 and (3 * d_attn + d_pool) % d_mem == 0

    x = jnp.concatenate([x_prompt.reshape(mp, d), x_sample.reshape(ms, d)], axis=0)
    xb = x.astype(BF16)
    mem_b = mem_prompt.reshape(batch * n_mem, d).astype(BF16)
    b_exp_in4 = b_exp_in.reshape(depth, n_exp, 1, -1)
    b_exp_out4 = b_exp_out.reshape(depth, n_exp, 1, -1)

    outs = {k: [] for k in ("bkp", "bvp", "pp", "mkp", "mvp", "bks", "bvs", "ps")}
    for l in range(depth):
        h = _matmul(xb, w_in[l].astype(BF16), F32, "in_proj")
        mem_kv = _matmul(mem_b, w_mem_kv[l].astype(BF16), F32, "mem_kv")

        bias_p = _band_bias_prompt(rel_bias[l], qrows, nkb)
        bias_c, bias_n = _band_bias_sample(rel_bias[l], dseq, n_cached)
        attn_p = _band_prompt(h, bias_p, batch=batch, seq=seq, d_attn=d_attn, heads_per_step=heads_per_step)
        attn_s = _band_sample(h, cache_band_k[l].reshape(dbatch, n_cached, d_attn),
                              cache_band_v[l].reshape(dbatch, n_cached, d_attn), bias_c, bias_n,
                              row0=mp, batch=dbatch, seq=dseq, d_attn=d_attn, heads_per_step=heads_per_step)
        attn = jnp.concatenate([attn_p, attn_s], axis=0)

        mem_p = _mem_attn(h, mem_kv, mem_kv, row0=0, batch=batch, seq=seq, d_mem=d_mem,
                          q_col_blk=qm_col_blk, k_col_blk=0, v_col_blk=1, n_mem=n_mem, name="mem_prompt")
        mem_s = _mem_attn(h, cache_mem_k[l].reshape(dbatch * n_mem, d_mem),
                          cache_mem_v[l].reshape(dbatch * n_mem, d_mem), row0=mp, batch=dbatch, seq=dseq,
                          d_mem=d_mem, q_col_blk=qm_col_blk, k_col_blk=0, v_col_blk=0, n_mem=n_mem,
                          name="mem_sample")
        mem = jnp.concatenate([mem_p, mem_s], axis=0)

        hist = jnp.pad(state_pool[l], ((0, 0), (HIST_ROWS - POOL_HIST, 0), (0, 0))).reshape(
            dbatch * HIST_ROWS, d_pool)
        mixed_p = _pool(h, None, row0=0, batch=batch, seq=seq, d_pool=d_pool, u_col_blk=u_col_blk, pos0=0,
                        name="pool_prompt")
        mixed_s = _pool(h, hist, row0=mp, batch=dbatch, seq=dseq, d_pool=d_pool, u_col_blk=u_col_blk,
                        pos0=PAST_LEN, name="pool_sample")
        mixed = jnp.concatenate([mixed_p, mixed_s], axis=0)

        merged = _merge(xb, w_gate[l].astype(BF16), b_gate[l].reshape(1, -1), attn,
                        w_attn_proj[l].astype(BF16), mixed,
                        w_pool_map[l].reshape(d_pool, -1).astype(BF16), pool_scale[l].reshape(1, -1),
                        mem, w_mem_proj[l].astype(BF16))
        pre_ln = _outproj_residual(merged, w_out[l].astype(BF16), x, alpha=alpha)
        x1, top_idx, top_w = _ln_router(pre_ln, ln_mix_g[l].reshape(1, -1), ln_mix_b[l].reshape(1, -1),
                                        w_router[l], b_router[l].reshape(1, -1))

        tables, n_tiles, n_super = _route_tables(top_idx[:, :TOP_K], n_exp)
        row_tok = _row_tokens(tables, n_tiles * ROW_TILE, n_exp)
        xs = _dispatch(x1, row_tok, tables["n_tiles_used"], n_tiles, ROW_TILE)
        act = _gemm1(xs, w_exp_in, b_exp_in4, l, tables, n_tiles, n_super)
        y = _gemm2(act, w_exp_out, b_exp_out4, l, tables, n_tiles, n_super)
        x, xb = _combine_ln(y, tables["pos"], x1, top_w, ln_ffn_g[l].reshape(1, -1),
                            ln_ffn_b[l].reshape(1, -1), alpha=alpha, tile=ROW_TILE)

        hp = h[:mp].reshape(batch, seq, -1)
        hs = h[mp:].reshape(dbatch, dseq, -1)
        keep = min(BAND_PREV * CHUNK, seq)
        outs["bkp"].append(hp[:, seq - keep:, d_attn:2 * d_attn].reshape(batch, keep, n_heads, HEAD_DIM))
        outs["bvp"].append(hp[:, seq - keep:, 2 * d_attn:3 * d_attn].reshape(batch, keep, n_heads, HEAD_DIM))
        u_p = hp[:, :, 3 * d_attn:3 * d_attn + d_pool]
        outs["pp"].append(jnp.concatenate(
            [jnp.zeros((batch, POOL_HIST, d_pool), F32), u_p], axis=1)[:, -POOL_HIST:])
        hd_mem = d_mem // MEM_HEADS
        outs["mkp"].append(mem_kv[:, :d_mem].reshape(batch, n_mem, MEM_HEADS, hd_mem))
        outs["mvp"].append(mem_kv[:, d_mem:].reshape(batch, n_mem, MEM_HEADS, hd_mem))
        outs["bks"].append(hs[:, :, d_attn:2 * d_attn].reshape(dbatch, dseq, n_heads, HEAD_DIM))
        outs["bvs"].append(hs[:, :, 2 * d_attn:3 * d_attn].reshape(dbatch, dseq, n_heads, HEAD_DIM))
        u_s = hs[:, :, 3 * d_attn:3 * d_attn + d_pool]
        outs["ps"].append(jnp.concatenate([state_pool[l], u_s], axis=1)[:, -POOL_HIST:])

    st = lambda k: jnp.stack(outs[k])
    return (x[:mp].reshape(batch, seq, d), x[mp:].reshape(dbatch, dseq, d),
            st("bkp"), st("bvp"), st("pp"), st("mkp"), st("mvp"), st("bks"), st("bvs"), st("ps"))
```

```python
import functools

import numpy as np
import jax
import jax.numpy as jnp
from jax import lax
from jax.experimental import pallas as pl
from jax.experimental.pallas import tpu as pltpu

CHUNK = 64
BAND_PREV = 8
HEAD_DIM = 128
POOL_WINDOWS = (2, 4, 8, 16)
POOL_HIST = max(POOL_WINDOWS) - 1
HIST_ROWS = POOL_HIST + 1
MEM_HEADS = 4
N_BRANCHES = 3
TOP_K = 4
SWIGLU_LIMIT = 7.0
SWIGLU_ALPHA = 1.702
LN_EPS = 1e-5
NEG_INF = -1e30
PAST_LEN = 4096

ROW_TILE = 256
SUPER_TILES = 6
COMBINE_TILE = 128
MIB = 1024 * 1024

F32 = jnp.float32
BF16 = jnp.bfloat16


def _pick(n, cands):
    for c in cands:
        if n % c == 0:
            return c
    return n


def _params(n_axes, vmem_mib):
    return pltpu.CompilerParams(
        dimension_semantics=("arbitrary",) * n_axes, vmem_limit_bytes=int(vmem_mib * MIB))


def _dot(a, b):
    return jnp.dot(a, b, preferred_element_type=F32)


def _dot_t(a, b):
    return lax.dot_general(a, b, (((1,), (1,)), ((), ())), preferred_element_type=F32)


def _mm_body(a_ref, w_ref, o_ref):
    o_ref[...] = _dot(a_ref[...], w_ref[...]).astype(o_ref.dtype)


def _matmul(a, w, out_dtype, name):
    m, k = a.shape
    n = w.shape[1]
    tm = _pick(m, (1088, 1024, 512, 256, 128))
    tn = _pick(n, (1024, 512, 256, 128))
    return pl.pallas_call(
        _mm_body,
        grid=(m // tm, n // tn),
        in_specs=[pl.BlockSpec((tm, k), lambda i, j: (i, 0)),
                  pl.BlockSpec((k, tn), lambda i, j: (0, j))],
        out_specs=pl.BlockSpec((tm, tn), lambda i, j: (i, j)),
        out_shape=jax.ShapeDtypeStruct((m, n), out_dtype),
        compiler_params=_params(2, 56),
        name=name,
    )(a, w)


def _softmax_pv(scores, values):
    m = scores[0].max(axis=-1, keepdims=True)
    for s in scores[1:]:
        m = jnp.maximum(m, s.max(axis=-1, keepdims=True))
    es = [jnp.exp(s - m) for s in scores]
    l = es[0].sum(axis=-1, keepdims=True)
    for e in es[1:]:
        l = l + e.sum(axis=-1, keepdims=True)
    inv = 1.0 / l
    o = None
    for e, v in zip(es, values):
        oj = _dot((e * inv).astype(BF16), v)
        o = oj if o is None else o + oj
    return o


def _band_prompt_body(*refs, heads, nkb, qrows, scale):
    q_ref = refs[0]
    k_refs = refs[1:1 + nkb]
    v_refs = refs[1 + nkb:1 + 2 * nkb]
    bias_ref = refs[1 + 2 * nkb]
    o_ref = refs[2 + 2 * nkb]
    qb = pl.program_id(2)
    for h in range(heads):
        sl = slice(h * HEAD_DIM, (h + 1) * HEAD_DIM)
        q = q_ref[:, sl].astype(BF16)
        scores = []
        for j in range(nkb):
            s = _dot_t(q, k_refs[j][:, sl].astype(BF16)) * scale
            s = s + bias_ref[h, :, j * qrows:(j + 1) * qrows]
            if j < nkb - 1:
                s = jnp.where(qb >= nkb - 1 - j, s, NEG_INF)
            scores.append(s)
        o = _softmax_pv(scores, [v_refs[j][:, sl].astype(BF16) for j in range(nkb)])
        o_ref[:, sl] = o.astype(o_ref.dtype)


def _band_prompt(h, bias, *, batch, seq, d_attn, heads_per_step):
    qrows = ROW_TILE
    nkb = BAND_PREV * CHUNK // qrows + 1
    nq = seq // qrows
    gw = heads_per_step * HEAD_DIM
    hg = d_attn // gw
    q_spec = pl.BlockSpec((qrows, gw), lambda g, b, t: (b * nq + t, g))

    def kv_spec(j, col0):
        back = nkb - 1 - j
        return pl.BlockSpec((qrows, gw),
                            lambda g, b, t: (b * nq + jnp.maximum(t - back, 0), col0 + g))

    in_specs = ([q_spec] + [kv_spec(j, hg) for j in range(nkb)]
                + [kv_spec(j, 2 * hg) for j in range(nkb)]
                + [pl.BlockSpec((heads_per_step, qrows, nkb * qrows), lambda g, b, t: (g, 0, 0))])
    body = functools.partial(_band_prompt_body, heads=heads_per_step, nkb=nkb, qrows=qrows,
                             scale=HEAD_DIM ** -0.5)
    return pl.pallas_call(
        body,
        grid=(hg, batch, nq),
        in_specs=in_specs,
        out_specs=pl.BlockSpec((qrows, gw), lambda g, b, t: (b * nq + t, g)),
        out_shape=jax.ShapeDtypeStruct((batch * seq, d_attn), BF16),
        compiler_params=_params(3, 40),
        name="band_prompt",
    )(*([h] * (1 + 2 * nkb)), bias)


def _band_sample_body(q_ref, kc_ref, kn_ref, vc_ref, vn_ref, bc_ref, bn_ref, o_ref, *, heads, scale):
    for h in range(heads):
        sl = slice(h * HEAD_DIM, (h + 1) * HEAD_DIM)
        q = q_ref[:, sl].astype(BF16)
        sc = _dot_t(q, kc_ref[:, sl].astype(BF16)) * scale + bc_ref[h]
        sn = _dot_t(q, kn_ref[:, sl].astype(BF16)) * scale + bn_ref[h]
        o = _softmax_pv([sc, sn], [vc_ref[:, sl].astype(BF16), vn_ref[:, sl].astype(BF16)])
        o_ref[:, sl] = o.astype(o_ref.dtype)


def _band_sample(h, cache_k, cache_v, bias_c, bias_n, *, row0, batch, seq, d_attn, heads_per_step):
    n_cached = cache_k.shape[1]
    gw = heads_per_step * HEAD_DIM
    hg = d_attn // gw
    rb0 = row0 // seq
    new_spec = lambda col0: pl.BlockSpec((seq, gw), lambda g, b: (rb0 + b, col0 + g))
    cache_spec = pl.BlockSpec((None, n_cached, gw), lambda g, b: (b, 0, g))
    body = functools.partial(_band_sample_body, heads=heads_per_step, scale=HEAD_DIM ** -0.5)
    return pl.pallas_call(
        body,
        grid=(hg, batch),
        in_specs=[new_spec(0), cache_spec, new_spec(hg), cache_spec, new_spec(2 * hg),
                  pl.BlockSpec((heads_per_step, seq, n_cached), lambda g, b: (g, 0, 0)),
                  pl.BlockSpec((heads_per_step, seq, seq), lambda g, b: (g, 0, 0))],
        out_specs=pl.BlockSpec((seq, gw), lambda g, b: (b, g)),
        out_shape=jax.ShapeDtypeStruct((batch * seq, d_attn), BF16),
        compiler_params=_params(2, 40),
        name="band_sample",
    )(h, cache_k, h, cache_v, h, bias_c, bias_n)


def _toeplitz_bias(rel_bias, rows, width, off):
    max_rel = (rel_bias.shape[1] - 1) // 2
    n = 1 << int(np.ceil(np.log2(width + rows)))
    u = np.arange(n)
    lag = np.where(u < width, u, u - n)
    rel = np.clip(off - lag, -max_rel, max_rel) + max_rel
    row = rel_bias[:, jnp.asarray(rel)].astype(F32)
    skew = jnp.tile(row, (1, rows))[:, :rows * (n - 1)].reshape(row.shape[0], rows, n - 1)
    return skew[:, :, :width]


def _band_bias_prompt(rel_bias, qrows, nkb):
    i = np.arange(qrows)[:, None]
    j = np.arange(nkb * qrows)[None, :]
    cq, ck = i // CHUNK, j // CHUNK
    in_band = (ck >= cq) & (ck <= cq + BAND_PREV)
    bias = _toeplitz_bias(rel_bias, qrows, nkb * qrows, (nkb - 1) * qrows)
    return jnp.where(jnp.asarray(in_band)[None], bias, NEG_INF)


def _band_bias_sample(rel_bias, seq, n_cached):
    bias = _toeplitz_bias(rel_bias, seq, n_cached + seq, n_cached)
    return bias[:, :, :n_cached], bias[:, :, n_cached:]


def _mem_body(q_ref, k_ref, v_ref, o_ref, *, heads, hd, scale):
    for h in range(heads):
        sl = slice(h * hd, (h + 1) * hd)
        s = _dot_t(q_ref[:, sl].astype(BF16), k_ref[:, sl].astype(BF16)) * scale
        o = _softmax_pv([s], [v_ref[:, sl].astype(BF16)])
        o_ref[:, sl] = o.astype(o_ref.dtype)


def _mem_attn(h, kmat, vmat, *, row0, batch, seq, d_mem, q_col_blk, k_col_blk, v_col_blk, n_mem, name):
    tq = _pick(seq, (512, 256, 128, 64, 32))
    nt = seq // tq
    rb0 = row0 // tq
    hd = d_mem // MEM_HEADS
    body = functools.partial(_mem_body, heads=MEM_HEADS, hd=hd, scale=hd ** -0.5)
    return pl.pallas_call(
        body,
        grid=(batch, nt),
        in_specs=[pl.BlockSpec((tq, d_mem), lambda b, t: (rb0 + b * nt + t, q_col_blk)),
                  pl.BlockSpec((n_mem, d_mem), lambda b, t: (b, k_col_blk)),
                  pl.BlockSpec((n_mem, d_mem), lambda b, t: (b, v_col_blk))],
        out_specs=pl.BlockSpec((tq, d_mem), lambda b, t: (b * nt + t, 0)),
        out_shape=jax.ShapeDtypeStruct((batch * seq, d_mem), BF16),
        compiler_params=_params(2, 40),
        name=name,
    )(h, kmat, vmat)


def _pool_body(u_ref, prev_ref, o_ref, ext_ref, *, rows, gc, pos0, zero_first_hist):
    t = pl.program_id(1)
    prev = prev_ref[...]
    if zero_first_hist:
        prev = jnp.where(t == 0, 0.0, prev)
    ext_ref[0:HIST_ROWS, :] = prev
    ext_ref[HIST_ROWS:HIST_ROWS + rows, :] = u_ref[...]
    pos = pos0 + t * rows + lax.broadcasted_iota(jnp.int32, (rows, 1), 0)
    for g, w in enumerate(POOL_WINDOWS):
        cs = slice(g * gc, (g + 1) * gc)
        acc = ext_ref[HIST_ROWS:HIST_ROWS + rows, cs]
        for back in range(1, w):
            acc = acc + ext_ref[HIST_ROWS - back:HIST_ROWS - back + rows, cs]
        cnt = jnp.minimum(w, pos + 1).astype(F32)
        o_ref[:, cs] = (acc / cnt - u_ref[:, cs]).astype(o_ref.dtype)


def _pool(h, hist, *, row0, batch, seq, d_pool, u_col_blk, pos0, name):
    rows = _pick(seq, (ROW_TILE, 128, 64, 32))
    nt = seq // rows
    rb0 = row0 // rows
    gc = d_pool // len(POOL_WINDOWS)
    if hist is None:
        per = rows // HIST_ROWS
        prev_arr = h
        prev_spec = pl.BlockSpec(
            (HIST_ROWS, d_pool),
            lambda b, t: (jnp.maximum((rb0 + b * nt + t) * per - 1, 0), u_col_blk))
    else:
        assert nt == 1
        prev_arr = hist
        prev_spec = pl.BlockSpec((HIST_ROWS, d_pool), lambda b, t: (b, 0))
    body = functools.partial(_pool_body, rows=rows, gc=gc, pos0=pos0, zero_first_hist=hist is None)
    return pl.pallas_call(
        body,
        grid=(batch, nt),
        in_specs=[pl.BlockSpec((rows, d_pool), lambda b, t: (rb0 + b * nt + t, u_col_blk)), prev_spec],
        out_specs=pl.BlockSpec((rows, d_pool), lambda b, t: (b * nt + t, 0)),
        out_shape=jax.ShapeDtypeStruct((batch * seq, d_pool), BF16),
        scratch_shapes=[pltpu.VMEM((HIST_ROWS + rows, d_pool), F32)],
        compiler_params=_params(2, 32),
        name=name,
    )(h, prev_arr)


def _merge_body(x_ref, wg0, wg1, wg2, bg0, bg1, bg2, a_ref, wa_ref, p_ref, wp_ref, ps_ref,
                m_ref, wm_ref, o_ref):
    x = x_ref[...]

    def gate(w_ref, b_ref):
        return jax.nn.sigmoid(_dot(x, w_ref[...]) + b_ref[...])

    br_attn = _dot(a_ref[...], wa_ref[...])
    br_pool = _dot(p_ref[...], wp_ref[...]) * ps_ref[...]
    br_mem = _dot(m_ref[...], wm_ref[...])
    merged = gate(wg0, bg0) * br_attn + gate(wg1, bg1) * br_pool + gate(wg2, bg2) * br_mem
    o_ref[...] = merged.astype(o_ref.dtype)


def _merge(xb, w_gate, b_gate, attn, w_attn, mixed, w_pool, pool_scale, mem, w_mem):
    m, d = xb.shape
    d_attn, d_pool, d_mem = attn.shape[1], mixed.shape[1], mem.shape[1]
    gc = d_pool // len(POOL_WINDOWS)
    pool_out = w_pool.shape[1]
    tm = _pick(m, (512, 256, 128))
    tn = _pick(pool_out, (256, 128))
    nj = d // tn
    per_group = pool_out // tn
    row = lambda i, j: (i, 0)
    gate_w = lambda br: pl.BlockSpec((d, tn), lambda i, j: (0, br * nj + j))
    gate_b = lambda br: pl.BlockSpec((1, tn), lambda i, j: (0, br * nj + j))
    return pl.pallas_call(
        _merge_body,
        grid=(m // tm, nj),
        in_specs=[pl.BlockSpec((tm, d), row), gate_w(0), gate_w(1), gate_w(2),
                  gate_b(0), gate_b(1), gate_b(2),
                  pl.BlockSpec((tm, d_attn), row), pl.BlockSpec((d_attn, tn), lambda i, j: (0, j)),
                  pl.BlockSpec((tm, gc), lambda i, j: (i, j // per_group)),
                  pl.BlockSpec((gc, tn), lambda i, j: (j // per_group, j % per_group)),
                  pl.BlockSpec((1, tn), lambda i, j: (0, j)),
                  pl.BlockSpec((tm, d_mem), row), pl.BlockSpec((d_mem, tn), lambda i, j: (0, j))],
        out_specs=pl.BlockSpec((tm, tn), lambda i, j: (i, j)),
        out_shape=jax.ShapeDtypeStruct((m, d), BF16),
        compiler_params=_params(2, 48),
        name="merge",
    )(xb, w_gate, w_gate, w_gate, b_gate, b_gate, b_gate, attn, w_attn, mixed, w_pool, pool_scale,
      mem, w_mem)


def _layer_norm_rows(y, g, b):
    mu = jnp.mean(y, axis=-1, keepdims=True)
    yc = y - mu
    var = jnp.mean(yc * yc, axis=-1, keepdims=True)
    return yc * lax.rsqrt(var + LN_EPS) * g + b


def _outproj_body(a_ref, w_ref, x_ref, o_ref, *, alpha):
    o_ref[...] = alpha * x_ref[...] + _dot(a_ref[...], w_ref[...])


def _outproj_residual(merged, w_out, x, *, alpha):
    m, d = x.shape
    tm = _pick(m, (544, 512, 256, 128))
    tn = _pick(d, (1024, 512, 256, 128))
    return pl.pallas_call(
        functools.partial(_outproj_body, alpha=alpha),
        grid=(m // tm, d // tn),
        in_specs=[pl.BlockSpec((tm, d), lambda i, j: (i, 0)), pl.BlockSpec((d, tn), lambda i, j: (0, j)),
                  pl.BlockSpec((tm, tn), lambda i, j: (i, j))],
        out_specs=pl.BlockSpec((tm, tn), lambda i, j: (i, j)),
        out_shape=jax.ShapeDtypeStruct((m, d), F32),
        compiler_params=_params(2, 48),
        name="out_proj",
    )(merged, w_out, x)


def _ln_router_body(y_ref, g_ref, b_ref, wr_ref, br_ref, x1_ref, idx_ref, wgt_ref, *, n_exp):
    tm = y_ref.shape[0]
    xn = _layer_norm_rows(y_ref[...], g_ref[...], b_ref[...])
    x1_ref[...] = xn
    logits = _dot(xn.astype(BF16), wr_ref[...].astype(BF16)) + br_ref[...]

    col = lax.broadcasted_iota(jnp.int32, (tm, n_exp), 1).astype(F32)
    vals, idxs = [], []
    for _ in range(TOP_K):
        mx = logits.max(axis=-1, keepdims=True)
        ix = jnp.where(logits == mx, col, float(n_exp)).min(axis=-1, keepdims=True)
        vals.append(mx)
        idxs.append(ix)
        logits = jnp.where(col == ix, -jnp.inf, logits)
    es = [jnp.exp(v - vals[0]) for v in vals]
    den = es[0]
    for e in es[1:]:
        den = den + e
    lane = lax.broadcasted_iota(jnp.int32, idx_ref.shape, 1)
    idx_out = jnp.zeros(idx_ref.shape, F32)
    wgt_out = jnp.zeros(wgt_ref.shape, F32)
    for k in range(TOP_K):
        idx_out = jnp.where(lane == k, idxs[k], idx_out)
        wgt_out = jnp.where(lane == k, es[k] / den, wgt_out)
    idx_ref[...] = idx_out.astype(jnp.int32)
    wgt_ref[...] = wgt_out


def _ln_router(y, ln_g, ln_b, w_router, b_router):
    m, d = y.shape
    n_exp = w_router.shape[1]
    tm = ROW_TILE
    row = lambda i: (i, 0)
    const = lambda i: (0, 0)
    return pl.pallas_call(
        functools.partial(_ln_router_body, n_exp=n_exp),
        grid=(m // tm,),
        in_specs=[pl.BlockSpec((tm, d), row), pl.BlockSpec((1, d), const), pl.BlockSpec((1, d), const),
                  pl.BlockSpec((d, n_exp), const), pl.BlockSpec((1, n_exp), const)],
        out_specs=[pl.BlockSpec((tm, d), row), pl.BlockSpec((tm, 128), row), pl.BlockSpec((tm, 128), row)],
        out_shape=[jax.ShapeDtypeStruct((m, d), F32), jax.ShapeDtypeStruct((m, 128), jnp.int32),
                   jax.ShapeDtypeStruct((m, 128), F32)],
        compiler_params=_params(1, 40),
        name="ln_router",
    )(y, ln_g, ln_b, w_router, b_router)


def _route_tables(top_idx, n_exp):
    m, k = top_idx.shape
    i32 = jnp.int32
    e = top_idx.reshape(-1)
    onehot = (e[:, None] == jnp.arange(n_exp, dtype=i32)[None, :]).astype(i32)
    csum = jnp.cumsum(onehot, axis=0)
    rank = jnp.sum(csum * onehot, axis=1) - 1
    counts = csum[-1]
    ntile = (counts + ROW_TILE - 1) // ROW_TILE
    tend = jnp.cumsum(ntile)
    tstart = tend - ntile
    pos = jnp.sum(onehot * (tstart * ROW_TILE)[None, :], axis=1) + rank
    n_tiles = (m * k) // ROW_TILE + n_exp
    n_super = (m * k) // (ROW_TILE * SUPER_TILES) + n_exp

    nsup = (ntile + SUPER_TILES - 1) // SUPER_TILES
    send = jnp.cumsum(nsup)
    sstart = send - nsup
    six = jnp.arange(n_super, dtype=i32)
    se = jnp.minimum(jnp.sum((six[:, None] >= send[None, :]).astype(i32), axis=1), n_exp - 1)
    live = six < send[-1]
    se = jnp.where(live, se, se[jnp.maximum(send[-1] - 1, 0)])
    sub = (six - sstart[se]) * SUPER_TILES
    tile0 = jnp.where(live, tstart[se] + sub, 0)
    ntile_s = jnp.where(live, jnp.minimum(ntile[se] - sub, SUPER_TILES), 0)
    tables = dict(
        pos=pos.astype(i32), se=se.astype(i32), tile0=tile0.astype(i32), ntile_s=ntile_s.astype(i32),
        n_super_used=send[-1].reshape(1).astype(i32), n_tiles_used=tend[-1].reshape(1).astype(i32),
        valid_end=(tstart * ROW_TILE + counts).astype(i32), pad_end=(tend * ROW_TILE).astype(i32))
    return tables, n_tiles, n_super


def _rowtok_body(pos_ref, vend_ref, pend_ref, ntu_ref, rt_ref, *, n_pairs, n_rows, n_exp):
    def clear(i, c):
        rt_ref[i] = 0
        return c

    def clear_pad(e, c):
        return lax.fori_loop(vend_ref[e], pend_ref[e], clear, c)

    lax.fori_loop(0, n_exp, clear_pad, 0)
    lax.fori_loop(ntu_ref[0] * ROW_TILE, n_rows, clear, 0)

    def put(i, c):
        rt_ref[pos_ref[i]] = lax.div(i, TOP_K)
        return c

    lax.fori_loop(0, n_pairs, put, 0, unroll=8)


def _row_tokens(tables, n_rows, n_exp):
    pos = tables["pos"]
    body = functools.partial(_rowtok_body, n_pairs=pos.shape[0], n_rows=n_rows, n_exp=n_exp)
    smem = pl.BlockSpec(memory_space=pltpu.SMEM)
    return pl.pallas_call(
        body,
        in_specs=[smem, smem, smem, smem],
        out_specs=smem,
        out_shape=jax.ShapeDtypeStruct((n_rows,), jnp.int32),
        name="row_tokens",
    )(pos, tables["valid_end"], tables["pad_end"], tables["n_tiles_used"])


def _row_copy(src_hbm, src_row, dst, dst_row, sem):
    return pltpu.make_async_copy(src_hbm.at[pl.ds(src_row, 1), :], dst.at[pl.ds(dst_row, 1), :], sem)


def _gather_body(rt_ref, nu_ref, x_hbm, o_ref, buf, sem, *, tile):
    i = pl.program_id(0)
    nu = nu_ref[0]
    slot = lax.rem(i, 2)

    def issue(t, sl):
        def one(r, c):
            _row_copy(x_hbm, rt_ref[t * tile + r], buf.at[sl], r, sem.at[sl]).start()
            return c

        lax.fori_loop(0, tile, one, 0, unroll=8)

    def wait(sl):
        def one(r, c):
            _row_copy(x_hbm, 0, buf.at[sl], r, sem.at[sl]).wait()
            return c

        lax.fori_loop(0, tile, one, 0, unroll=8)

    @pl.when(i == 0)
    def _():
        issue(0, 0)

    @pl.when(i < nu)
    def _():
        wait(slot)

        @pl.when(i + 1 < nu)
        def _():
            issue(i + 1, 1 - slot)

        o_ref[...] = buf[slot].astype(o_ref.dtype)

    @pl.when(i >= nu)
    def _():
        o_ref[...] = jnp.zeros(o_ref.shape, o_ref.dtype)


def _dispatch(x1, row_tok, n_used, n_tiles, tile):
    m, d = x1.shape
    body = functools.partial(_gather_body, tile=tile)
    grid_spec = pltpu.PrefetchScalarGridSpec(
        num_scalar_prefetch=2,
        grid=(n_tiles,),
        in_specs=[pl.BlockSpec(memory_space=pl.ANY)],
        out_specs=pl.BlockSpec((tile, d), lambda i, rt, nu: (i, 0)),
        scratch_shapes=[pltpu.VMEM((2, tile, d), F32), pltpu.SemaphoreType.DMA((2,))],
    )
    return pl.pallas_call(
        body,
        grid_spec=grid_spec,
        out_shape=jax.ShapeDtypeStruct((n_tiles * tile, d), BF16),
        compiler_params=_params(1, 32),
        name="moe_dispatch",
    )(row_tok, n_used, x1)


def _grouped_body(se_ref, t0_ref, nt_ref, nsu_ref, ntu_ref, x_hbm, *rest,
                  n_w, tn, nj, n_super, n_tiles, epilogue):
    w_refs, b_refs = rest[:n_w], rest[n_w:2 * n_w]
    out_hbm = rest[2 * n_w]
    xres, wb, ostage, xsem, osem, flags = rest[2 * n_w + 1:]
    s, j = pl.program_id(0), pl.program_id(1)
    nsu = nsu_ref[0]
    live = s < nsu
    slot = lax.rem(s, 2)

    def rows(t):
        return pl.ds(pl.multiple_of(t * ROW_TILE, ROW_TILE), ROW_TILE)

    def lhs_copy(tile, q, sl):
        return pltpu.make_async_copy(x_hbm.at[rows(tile), :], xres.at[sl, rows(q), :], xsem.at[sl])

    def issue_lhs(sup, sl):
        def one(q, c):
            lhs_copy(t0_ref[sup] + q, q, sl).start()
            return c

        lax.fori_loop(0, nt_ref[sup], one, 0)

    def wait_lhs(sup, sl):
        def one(q, c):
            lhs_copy(0, q, sl).wait()
            return c

        lax.fori_loop(0, nt_ref[sup], one, 0)

    def out_copy(p, half, tile, col0):
        return pltpu.make_async_copy(ostage.at[p, pl.ds(half * ROW_TILE, ROW_TILE), :],
                                     out_hbm.at[rows(tile), pl.ds(col0, tn)], osem.at[p])

    def drain(p):
        def one(i, c):
            out_copy(p, 0, 0, 0).wait()
            return c

        lax.fori_loop(0, flags[p], one, 0)
        flags[p] = 0

    @pl.when((s == 0) & (j == 0))
    def _():
        flags[0] = 0
        flags[1] = 0
        issue_lhs(0, 0)

    @pl.when(live & (j == 0))
    def _():
        wait_lhs(s, slot)

        @pl.when(s + 1 < nsu)
        def _():
            issue_lhs(s + 1, 1 - slot)

    @pl.when(live)
    def _():
        for k, w_ref in enumerate(w_refs):
            wb[:, k * tn:(k + 1) * tn] = w_ref[...].astype(BF16)
        col0 = pl.multiple_of(j * tn, tn)

        nt = nt_ref[s]

        def emit(q, n_sub, p):
            n_row = n_sub * ROW_TILE
            x = xres[slot, pl.ds(pl.multiple_of(q * ROW_TILE, ROW_TILE), n_row), :]
            res = epilogue(_dot(x, wb[...]), [b[...] for b in b_refs])
            drain(p)
            ostage[p, 0:n_row, :] = res.astype(ostage.dtype)
            for half in range(n_sub):
                out_copy(p, half, t0_ref[s] + q + half, col0).start()
            flags[p] = n_sub

        def pair(i, c):
            emit(2 * i, 2, lax.rem(i, 2))
            return c

        n_pair = lax.div(nt, 2)
        lax.fori_loop(0, n_pair, pair, 0)

        @pl.when(lax.rem(nt, 2) == 1)
        def _():
            emit(nt - 1, 1, lax.rem(n_pair, 2))

    @pl.when((s == n_super - 1) & (j == nj - 1))
    def _():
        drain(0)
        drain(1)
        ostage[0] = jnp.zeros(ostage.shape[1:], ostage.dtype)

        def zero_tile(t, c):
            for jj in range(nj):
                out_copy(0, 0, t, jj * tn).start()
            for jj in range(nj):
                out_copy(0, 0, 0, 0).wait()
            return c

        lax.fori_loop(ntu_ref[0], n_tiles, zero_tile, 0)


def _grouped_gemm(xs, weights, biases, layer, col_blocks, tables, n_tiles, n_super, *,
                  tn, n_out, out_dtype, epilogue, name):
    n_rows, d = xs.shape
    n_w = len(weights)
    nj = n_out // tn

    def wmap(blk0):
        def index(s, j, se, t0, nt, nsu, ntu):
            return (layer, se[s], 0, blk0 + jnp.where(s < nsu[0], j, nj - 1))
        return index

    in_specs = [pl.BlockSpec(memory_space=pl.ANY)]
    in_specs += [pl.BlockSpec((None, None, d, tn), wmap(b0)) for b0 in col_blocks]
    in_specs += [pl.BlockSpec((None, None, 1, tn), wmap(b0)) for b0 in col_blocks]
    body = functools.partial(_grouped_body, n_w=n_w, tn=tn, nj=nj, n_super=n_super, n_tiles=n_tiles,
                             epilogue=epilogue)
    grid_spec = pltpu.PrefetchScalarGridSpec(
        num_scalar_prefetch=5,
        grid=(n_super, nj),
        in_specs=in_specs,
        out_specs=pl.BlockSpec(memory_space=pl.ANY),
        scratch_shapes=[pltpu.VMEM((2, SUPER_TILES * ROW_TILE, d), BF16),
                        pltpu.VMEM((d, n_w * tn), BF16),
                        pltpu.VMEM((2, 2 * ROW_TILE, tn), out_dtype),
                        pltpu.SemaphoreType.DMA((2,)), pltpu.SemaphoreType.DMA((2,)),
                        pltpu.SMEM((2,), jnp.int32)],
    )
    return pl.pallas_call(
        body,
        grid_spec=grid_spec,
        out_shape=jax.ShapeDtypeStruct((n_rows, n_out), out_dtype),
        compiler_params=_params(2, 56),
        name=name,
    )(tables["se"], tables["tile0"], tables["ntile_s"], tables["n_super_used"], tables["n_tiles_used"],
      xs, *weights, *biases)


def _swiglu_epilogue(h, biases):
    tn = h.shape[1] // 2
    gate = jnp.minimum(h[:, :tn] + biases[0], SWIGLU_LIMIT)
    up = jnp.clip(h[:, tn:] + biases[1], -SWIGLU_LIMIT, SWIGLU_LIMIT)
    glu = gate * jax.nn.sigmoid(SWIGLU_ALPHA * gate)
    return glu * (up + 1.0)


def _bias_epilogue(h, biases):
    return h + biases[0]


def _gemm1(xs, w_in, b_in, layer, tables, n_tiles, n_super):
    d_ff = w_in.shape[3] // 2
    tn = _pick(d_ff, (256, 128))
    return _grouped_gemm(xs, [w_in, w_in], [b_in, b_in], layer, [0, d_ff // tn], tables, n_tiles, n_super,
                         tn=tn, n_out=d_ff, out_dtype=BF16, epilogue=_swiglu_epilogue, name="moe_gemm1")


def _gemm2(act, w_out, b_out, layer, tables, n_tiles, n_super):
    d = w_out.shape[3]
    tn = _pick(d, (512, 256, 128))
    return _grouped_gemm(act, [w_out], [b_out], layer, [0], tables, n_tiles, n_super,
                         tn=tn, n_out=d, out_dtype=F32, epilogue=_bias_epilogue, name="moe_gemm2")


def _combine_body(pos_ref, y_hbm, x1_ref, w_ref, g_ref, b_ref, x2_ref, x2b_ref, buf, sem, *, tile, alpha):
    i = pl.program_id(0)
    slot = lax.rem(i, 2)

    def issue(t, sl):
        base = t * tile * TOP_K

        def one(r, c):
            for k in range(TOP_K):
                _row_copy(y_hbm, pos_ref[base + r * TOP_K + k], buf.at[sl, k], r, sem.at[sl]).start()
            return c

        lax.fori_loop(0, tile, one, 0, unroll=2)

    def wait(sl):
        def one(r, c):
            for k in range(TOP_K):
                _row_copy(y_hbm, 0, buf.at[sl, k], r, sem.at[sl]).wait()
            return c

        lax.fori_loop(0, tile, one, 0, unroll=2)

    @pl.when(i == 0)
    def _():
        issue(0, 0)

    wait(slot)

    @pl.when(i + 1 < pl.num_programs(0))
    def _():
        issue(i + 1, 1 - slot)

    moe = w_ref[:, 0:1] * buf[slot, 0]
    for k in range(1, TOP_K):
        moe = moe + w_ref[:, k:k + 1] * buf[slot, k]
    out = _layer_norm_rows(alpha * x1_ref[...] + moe, g_ref[...], b_ref[...])
    x2_ref[...] = out
    x2b_ref[...] = out.astype(BF16)


def _combine_ln(y, pos, x1, top_w, ln_g, ln_b, *, alpha, tile):
    m, d = x1.shape
    body = functools.partial(_combine_body, tile=tile, alpha=alpha)
    row = lambda i, pos: (i, 0)
    const = lambda i, pos: (0, 0)
    grid_spec = pltpu.PrefetchScalarGridSpec(
        num_scalar_prefetch=1,
        grid=(m // tile,),
        in_specs=[pl.BlockSpec(memory_space=pl.ANY), pl.BlockSpec((tile, d), row),
                  pl.BlockSpec((tile, 128), row), pl.BlockSpec((1, d), const), pl.BlockSpec((1, d), const)],
        out_specs=[pl.BlockSpec((tile, d), row), pl.BlockSpec((tile, d), row)],
        scratch_shapes=[pltpu.VMEM((2, TOP_K, tile, d), F32), pltpu.SemaphoreType.DMA((2,))],
    )
    return pl.pallas_call(
        body,
        grid_spec=grid_spec,
        out_shape=[jax.ShapeDtypeStruct((m, d), F32), jax.ShapeDtypeStruct((m, d), BF16)],
        compiler_params=_params(1, 52),
        name="moe_combine_ln",
    )(pos, y, x1, top_w, ln_g, ln_b)


def kernel(x_prompt, x_sample, mem_prompt, cache_band_k, cache_band_v, state_pool, cache_mem_k, cache_mem_v, w_in, rel_bias, w_attn_proj, w_pool_map, pool_scale, w_mem_kv, w_mem_proj, w_gate, b_gate, w_out, ln_mix_g, ln_mix_b, w_router, b_router, w_exp_in, b_exp_in, w_exp_out, b_exp_out, ln_ffn_g, ln_ffn_b):
    batch, seq, d = x_prompt.shape
    dbatch, dseq, _ = x_sample.shape
    depth = w_in.shape[0]
    n_mem = mem_prompt.shape[1]
    d_attn = w_attn_proj.shape[1]
    d_mem = w_mem_proj.shape[1]
    d_pool = w_pool_map.shape[1] * w_pool_map.shape[2]
    n_heads = d_attn // HEAD_DIM
    n_exp = w_router.shape[2]
    n_cached = cache_band_k.shape[2]
    mp, ms = batch * seq, dbatch * dseq
    alpha = (2 * depth) ** 0.25
    heads_per_step = _pick(n_heads, (4, 2, 1))
    qrows = ROW_TILE
    nkb = BAND_PREV * CHUNK // qrows + 1
    u_col_blk = (3 * d_attn) // d_pool
    qm_col_blk = (3 * d_attn + d_pool) // d_mem
    assert seq % qrows == 0 and mp % dseq == 0 and (mp + ms) % ROW_TILE == 0
    assert (3 * d_attn) % d_pool == 0 and (3 * d_attn + d_pool) % d_mem == 0

    x = jnp.concatenate([x_prompt.reshape(mp, d), x_sample.reshape(ms, d)], axis=0)
    xb = x.astype(BF16)
    mem_b = mem_prompt.reshape(batch * n_mem, d).astype(BF16)
    b_exp_in4 = b_exp_in.reshape(depth, n_exp, 1, -1)
    b_exp_out4 = b_exp_out.reshape(depth, n_exp, 1, -1)

    outs = {k: [] for k in ("bkp", "bvp", "pp", "mkp", "mvp", "bks", "bvs", "ps")}
    for l in range(depth):
        h = _matmul(xb, w_in[l].astype(BF16), F32, "in_proj")
        mem_kv = _matmul(mem_b, w_mem_kv[l].astype(BF16), F32, "mem_kv")

        bias_p = _band_bias_prompt(rel_bias[l], qrows, nkb)
        bias_c, bias_n = _band_bias_sample(rel_bias[l], dseq, n_cached)
        attn_p = _band_prompt(h, bias_p, batch=batch, seq=seq, d_attn=d_attn, heads_per_step=heads_per_step)
        attn_s = _band_sample(h, cache_band_k[l].reshape(dbatch, n_cached, d_attn),
                              cache_band_v[l].reshape(dbatch, n_cached, d_attn), bias_c, bias_n,
                              row0=mp, batch=dbatch, seq=dseq, d_attn=d_attn, heads_per_step=heads_per_step)
        attn = jnp.concatenate([attn_p, attn_s], axis=0)

        mem_p = _mem_attn(h, mem_kv, mem_kv, row0=0, batch=batch, seq=seq, d_mem=d_mem,
                          q_col_blk=qm_col_blk, k_col_blk=0, v_col_blk=1, n_mem=n_mem, name="mem_prompt")
        mem_s = _mem_attn(h, cache_mem_k[l].reshape(dbatch * n_mem, d_mem),
                          cache_mem_v[l].reshape(dbatch * n_mem, d_mem), row0=mp, batch=dbatch, seq=dseq,
                          d_mem=d_mem, q_col_blk=qm_col_blk, k_col_blk=0, v_col_blk=0, n_mem=n_mem,
                          name="mem_sample")
        mem = jnp.concatenate([mem_p, mem_s], axis=0)

        hist = jnp.pad(state_pool[l], ((0, 0), (HIST_ROWS - POOL_HIST, 0), (0, 0))).reshape(
            dbatch * HIST_ROWS, d_pool)
        mixed_p = _pool(h, None, row0=0, batch=batch, seq=seq, d_pool=d_pool, u_col_blk=u_col_blk, pos0=0,
                        name="pool_prompt")
        mixed_s = _pool(h, hist, row0=mp, batch=dbatch, seq=dseq, d_pool=d_pool, u_col_blk=u_col_blk,
                        pos0=PAST_LEN, name="pool_sample")
        mixed = jnp.concatenate([mixed_p, mixed_s], axis=0)

        merged = _merge(xb, w_gate[l].astype(BF16), b_gate[l].reshape(1, -1), attn,
                        w_attn_proj[l].astype(BF16), mixed,
                        w_pool_map[l].reshape(d_pool, -1).astype(BF16), pool_scale[l].reshape(1, -1),
                        mem, w_mem_proj[l].astype(BF16))
        pre_ln = _outproj_residual(merged, w_out[l].astype(BF16), x, alpha=alpha)
        x1, top_idx, top_w = _ln_router(pre_ln, ln_mix_g[l].reshape(1, -1), ln_mix_b[l].reshape(1, -1),
                                        w_router[l], b_router[l].reshape(1, -1))

        tables, n_tiles, n_super = _route_tables(top_idx[:, :TOP_K], n_exp)
        row_tok = _row_tokens(tables, n_tiles * ROW_TILE, n_exp)
        xs = _dispatch(x1, row_tok, tables["n_tiles_used"], n_tiles, ROW_TILE)
        act = _gemm1(xs, w_exp_in, b_exp_in4, l, tables, n_tiles, n_super)
        y = _gemm2(act, w_exp_out, b_exp_out4, l, tables, n_tiles, n_super)
        x, xb = _combine_ln(y, tables["pos"], x1, top_w, ln_ffn_g[l].reshape(1, -1),
                            ln_ffn_b[l].reshape(1, -1), alpha=alpha, tile=COMBINE_TILE)

        hs = h[mp:].reshape(dbatch, dseq, -1)
        keep = min(BAND_PREV * CHUNK, seq)
        tail = max(keep, min(POOL_HIST, seq))
        hp = jnp.stack([h[(b + 1) * seq - tail:(b + 1) * seq, d_attn:3 * d_attn + d_pool] for b in range(batch)])
        outs["bkp"].append(hp[:, tail - keep:, :d_attn].reshape(batch, keep, n_heads, HEAD_DIM))
        outs["bvp"].append(hp[:, tail - keep:, d_attn:2 * d_attn].reshape(batch, keep, n_heads, HEAD_DIM))
        u_p = hp[:, :, 2 * d_attn:2 * d_attn + d_pool]
        outs["pp"].append(jnp.concatenate(
            [jnp.zeros((batch, POOL_HIST, d_pool), F32), u_p], axis=1)[:, -POOL_HIST:])
        hd_mem = d_mem // MEM_HEADS
        outs["mkp"].append(mem_kv[:, :d_mem].reshape(batch, n_mem, MEM_HEADS, hd_mem))
        outs["mvp"].append(mem_kv[:, d_mem:].reshape(batch, n_mem, MEM_HEADS, hd_mem))
        outs["bks"].append(hs[:, :, d_attn:2 * d_attn].reshape(dbatch, dseq, n_heads, HEAD_DIM))
        outs["bvs"].append(hs[:, :, 2 * d_attn:3 * d_attn].reshape(dbatch, dseq, n_heads, HEAD_DIM))
        u_s = hs[:, :, 3 * d_attn:3 * d_attn + d_pool]
        outs["ps"].append(jnp.concatenate([state_pool[l], u_s], axis=1)[:, -POOL_HIST:])

    st = lambda k: jnp.stack(outs[k])
    return (x[:mp].reshape(batch, seq, d), x[mp:].reshape(dbatch, dseq, d),
            st("bkp"), st("bvp"), st("pp"), st("mkp"), st("mvp"), st("bks"), st("bvs"), st("ps"))
```

```python
import functools

import numpy as np
import jax
import jax.numpy as jnp
from jax import lax
from jax.experimental import pallas as pl
from jax.experimental.pallas import tpu as pltpu

CHUNK = 64
BAND_PREV = 8
HEAD_DIM = 128
POOL_WINDOWS = (2, 4, 8, 16)
POOL_HIST = max(POOL_WINDOWS) - 1
HIST_ROWS = POOL_HIST + 1
MEM_HEADS = 4
N_BRANCHES = 3
TOP_K = 4
SWIGLU_LIMIT = 7.0
SWIGLU_ALPHA = 1.702
LN_EPS = 1e-5
NEG_INF = -1e30
PAST_LEN = 4096

ROW_TILE = 256
SUPER_TILES = 6
COMBINE_TILE = 128
MIB = 1024 * 1024

F32 = jnp.float32
BF16 = jnp.bfloat16


def _pick(n, cands):
    for c in cands:
        if n % c == 0:
            return c
    return n


def _params(n_axes, vmem_mib):
    return pltpu.CompilerParams(
        dimension_semantics=("arbitrary",) * n_axes, vmem_limit_bytes=int(vmem_mib * MIB))


def _dot(a, b):
    return jnp.dot(a, b, preferred_element_type=F32)


def _dot_t(a, b):
    return lax.dot_general(a, b, (((1,), (1,)), ((), ())), preferred_element_type=F32)


def _mm_body(a_ref, w_ref, o_ref):
    o_ref[...] = _dot(a_ref[...], w_ref[...]).astype(o_ref.dtype)


def _matmul(a, w, out_dtype, name):
    m, k = a.shape
    n = w.shape[1]
    tm = _pick(m, (1088, 1024, 512, 256, 128))
    tn = _pick(n, (1024, 512, 256, 128))
    return pl.pallas_call(
        _mm_body,
        grid=(m // tm, n // tn),
        in_specs=[pl.BlockSpec((tm, k), lambda i, j: (i, 0)),
                  pl.BlockSpec((k, tn), lambda i, j: (0, j))],
        out_specs=pl.BlockSpec((tm, tn), lambda i, j: (i, j)),
        out_shape=jax.ShapeDtypeStruct((m, n), out_dtype),
        compiler_params=_params(2, 56),
        name=name,
    )(a, w)


def _softmax_pv(scores, values):
    m = scores[0].max(axis=-1, keepdims=True)
    for s in scores[1:]:
        m = jnp.maximum(m, s.max(axis=-1, keepdims=True))
    es = [jnp.exp(s - m) for s in scores]
    l = es[0].sum(axis=-1, keepdims=True)
    for e in es[1:]:
        l = l + e.sum(axis=-1, keepdims=True)
    inv = 1.0 / l
    o = None
    for e, v in zip(es, values):
        oj = _dot((e * inv).astype(BF16), v)
        o = oj if o is None else o + oj
    return o


def _band_prompt_body(*refs, heads, nkb, qrows, scale):
    q_ref = refs[0]
    k_refs = refs[1:1 + nkb]
    v_refs = refs[1 + nkb:1 + 2 * nkb]
    bias_ref = refs[1 + 2 * nkb]
    o_ref = refs[2 + 2 * nkb]
    qb = pl.program_id(2)
    for h in range(heads):
        sl = slice(h * HEAD_DIM, (h + 1) * HEAD_DIM)
        q = q_ref[:, sl].astype(BF16)
        scores = []
        for j in range(nkb):
            s = _dot_t(q, k_refs[j][:, sl].astype(BF16)) * scale
            s = s + bias_ref[h, :, j * qrows:(j + 1) * qrows]
            if j < nkb - 1:
                s = jnp.where(qb >= nkb - 1 - j, s, NEG_INF)
            scores.append(s)
        o = _softmax_pv(scores, [v_refs[j][:, sl].astype(BF16) for j in range(nkb)])
        o_ref[:, sl] = o.astype(o_ref.dtype)


def _band_prompt(h, bias, *, batch, seq, d_attn, heads_per_step):
    qrows = ROW_TILE
    nkb = BAND_PREV * CHUNK // qrows + 1
    nq = seq // qrows
    gw = heads_per_step * HEAD_DIM
    hg = d_attn // gw
    q_spec = pl.BlockSpec((qrows, gw), lambda g, b, t: (b * nq + t, g))

    def kv_spec(j, col0):
        back = nkb - 1 - j
        return pl.BlockSpec((qrows, gw),
                            lambda g, b, t: (b * nq + jnp.maximum(t - back, 0), col0 + g))

    in_specs = ([q_spec] + [kv_spec(j, hg) for j in range(nkb)]
                + [kv_spec(j, 2 * hg) for j in range(nkb)]
                + [pl.BlockSpec((heads_per_step, qrows, nkb * qrows), lambda g, b, t: (g, 0, 0))])
    body = functools.partial(_band_prompt_body, heads=heads_per_step, nkb=nkb, qrows=qrows,
                             scale=HEAD_DIM ** -0.5)
    return pl.pallas_call(
        body,
        grid=(hg, batch, nq),
        in_specs=in_specs,
        out_specs=pl.BlockSpec((qrows, gw), lambda g, b, t: (b * nq + t, g)),
        out_shape=jax.ShapeDtypeStruct((batch * seq, d_attn), BF16),
        compiler_params=_params(3, 40),
        name="band_prompt",
    )(*([h] * (1 + 2 * nkb)), bias)


def _band_sample_body(q_ref, kc_ref, kn_ref, vc_ref, vn_ref, bc_ref, bn_ref, o_ref, *, heads, scale):
    for h in range(heads):
        sl = slice(h * HEAD_DIM, (h + 1) * HEAD_DIM)
        q = q_ref[:, sl].astype(BF16)
        sc = _dot_t(q, kc_ref[:, sl].astype(BF16)) * scale + bc_ref[h]
        sn = _dot_t(q, kn_ref[:, sl].astype(BF16)) * scale + bn_ref[h]
        o = _softmax_pv([sc, sn], [vc_ref[:, sl].astype(BF16), vn_ref[:, sl].astype(BF16)])
        o_ref[:, sl] = o.astype(o_ref.dtype)


def _band_sample(h, cache_k, cache_v, bias_c, bias_n, *, row0, batch, seq, d_attn, heads_per_step):
    n_cached = cache_k.shape[1]
    gw = heads_per_step * HEAD_DIM
    hg = d_attn // gw
    rb0 = row0 // seq
    new_spec = lambda col0: pl.BlockSpec((seq, gw), lambda g, b: (rb0 + b, col0 + g))
    cache_spec = pl.BlockSpec((None, n_cached, gw), lambda g, b: (b, 0, g))
    body = functools.partial(_band_sample_body, heads=heads_per_step, scale=HEAD_DIM ** -0.5)
    return pl.pallas_call(
        body,
        grid=(hg, batch),
        in_specs=[new_spec(0), cache_spec, new_spec(hg), cache_spec, new_spec(2 * hg),
                  pl.BlockSpec((heads_per_step, seq, n_cached), lambda g, b: (g, 0, 0)),
                  pl.BlockSpec((heads_per_step, seq, seq), lambda g, b: (g, 0, 0))],
        out_specs=pl.BlockSpec((seq, gw), lambda g, b: (b, g)),
        out_shape=jax.ShapeDtypeStruct((batch * seq, d_attn), BF16),
        compiler_params=_params(2, 40),
        name="band_sample",
    )(h, cache_k, h, cache_v, h, bias_c, bias_n)


def _toeplitz_bias(rel_bias, rows, width, off):
    max_rel = (rel_bias.shape[1] - 1) // 2
    n = 1 << int(np.ceil(np.log2(width + rows)))
    u = np.arange(n)
    lag = np.where(u < width, u, u - n)
    rel = np.clip(off - lag, -max_rel, max_rel) + max_rel
    row = rel_bias[:, jnp.asarray(rel)].astype(F32)
    skew = jnp.tile(row, (1, rows))[:, :rows * (n - 1)].reshape(row.shape[0], rows, n - 1)
    return skew[:, :, :width]


def _band_bias_prompt(rel_bias, qrows, nkb):
    i = np.arange(qrows)[:, None]
    j = np.arange(nkb * qrows)[None, :]
    cq, ck = i // CHUNK, j // CHUNK
    in_band = (ck >= cq) & (ck <= cq + BAND_PREV)
    bias = _toeplitz_bias(rel_bias, qrows, nkb * qrows, (nkb - 1) * qrows)
    return jnp.where(jnp.asarray(in_band)[None], bias, NEG_INF)


def _band_bias_sample(rel_bias, seq, n_cached):
    bias = _toeplitz_bias(rel_bias, seq, n_cached + seq, n_cached)
    return bias[:, :, :n_cached], bias[:, :, n_cached:]


def _mem_body(q_ref, k_ref, v_ref, o_ref, *, heads, hd, scale):
    for h in range(heads):
        sl = slice(h * hd, (h + 1) * hd)
        s = _dot_t(q_ref[:, sl].astype(BF16), k_ref[:, sl].astype(BF16)) * scale
        o = _softmax_pv([s], [v_ref[:, sl].astype(BF16)])
        o_ref[:, sl] = o.astype(o_ref.dtype)


def _mem_attn(h, kmat, vmat, *, row0, batch, seq, d_mem, q_col_blk, k_col_blk, v_col_blk, n_mem, name):
    tq = _pick(seq, (512, 256, 128, 64, 32))
    nt = seq // tq
    rb0 = row0 // tq
    hd = d_mem // MEM_HEADS
    body = functools.partial(_mem_body, heads=MEM_HEADS, hd=hd, scale=hd ** -0.5)
    return pl.pallas_call(
        body,
        grid=(batch, nt),
        in_specs=[pl.BlockSpec((tq, d_mem), lambda b, t: (rb0 + b * nt + t, q_col_blk)),
                  pl.BlockSpec((n_mem, d_mem), lambda b, t: (b, k_col_blk)),
                  pl.BlockSpec((n_mem, d_mem), lambda b, t: (b, v_col_blk))],
        out_specs=pl.BlockSpec((tq, d_mem), lambda b, t: (b * nt + t, 0)),
        out_shape=jax.ShapeDtypeStruct((batch * seq, d_mem), BF16),
        compiler_params=_params(2, 40),
        name=name,
    )(h, kmat, vmat)


def _pool_body(u_ref, prev_ref, o_ref, ext_ref, *, rows, gc, pos0, zero_first_hist):
    t = pl.program_id(1)
    prev = prev_ref[...]
    if zero_first_hist:
        prev = jnp.where(t == 0, 0.0, prev)
    ext_ref[0:HIST_ROWS, :] = prev
    ext_ref[HIST_ROWS:HIST_ROWS + rows, :] = u_ref[...]
    pos = pos0 + t * rows + lax.broadcasted_iota(jnp.int32, (rows, 1), 0)
    for g, w in enumerate(POOL_WINDOWS):
        cs = slice(g * gc, (g + 1) * gc)
        acc = ext_ref[HIST_ROWS:HIST_ROWS + rows, cs]
        for back in range(1, w):
            acc = acc + ext_ref[HIST_ROWS - back:HIST_ROWS - back + rows, cs]
        cnt = jnp.minimum(w, pos + 1).astype(F32)
        o_ref[:, cs] = (acc / cnt - u_ref[:, cs]).astype(o_ref.dtype)


def _pool(h, hist, *, row0, batch, seq, d_pool, u_col_blk, pos0, name):
    rows = _pick(seq, (ROW_TILE, 128, 64, 32))
    nt = seq // rows
    rb0 = row0 // rows
    gc = d_pool // len(POOL_WINDOWS)
    if hist is None:
        per = rows // HIST_ROWS
        prev_arr = h
        prev_spec = pl.BlockSpec(
            (HIST_ROWS, d_pool),
            lambda b, t: (jnp.maximum((rb0 + b * nt + t) * per - 1, 0), u_col_blk))
    else:
        assert nt == 1
        prev_arr = hist
        prev_spec = pl.BlockSpec((HIST_ROWS, d_pool), lambda b, t: (b, 0))
    body = functools.partial(_pool_body, rows=rows, gc=gc, pos0=pos0, zero_first_hist=hist is None)
    return pl.pallas_call(
        body,
        grid=(batch, nt),
        in_specs=[pl.BlockSpec((rows, d_pool), lambda b, t: (rb0 + b * nt + t, u_col_blk)), prev_spec],
        out_specs=pl.BlockSpec((rows, d_pool), lambda b, t: (b * nt + t, 0)),
        out_shape=jax.ShapeDtypeStruct((batch * seq, d_pool), BF16),
        scratch_shapes=[pltpu.VMEM((HIST_ROWS + rows, d_pool), F32)],
        compiler_params=_params(2, 32),
        name=name,
    )(h, prev_arr)


def _merge_body(x_ref, wg0, wg1, wg2, bg0, bg1, bg2, a_ref, wa_ref, p_ref, wp_ref, ps_ref,
                m_ref, wm_ref, o_ref):
    x = x_ref[...]

    def gate(w_ref, b_ref):
        return jax.nn.sigmoid(_dot(x, w_ref[...]) + b_ref[...])

    br_attn = _dot(a_ref[...], wa_ref[...])
    br_pool = _dot(p_ref[...], wp_ref[...]) * ps_ref[...]
    br_mem = _dot(m_ref[...], wm_ref[...])
    merged = gate(wg0, bg0) * br_attn + gate(wg1, bg1) * br_pool + gate(wg2, bg2) * br_mem
    o_ref[...] = merged.astype(o_ref.dtype)


def _merge(xb, w_gate, b_gate, attn, w_attn, mixed, w_pool, pool_scale, mem, w_mem):
    m, d = xb.shape
    d_attn, d_pool, d_mem = attn.shape[1], mixed.shape[1], mem.shape[1]
    gc = d_pool // len(POOL_WINDOWS)
    pool_out = w_pool.shape[1]
    tm = _pick(m, (512, 256, 128))
    tn = _pick(pool_out, (256, 128))
    nj = d // tn
    per_group = pool_out // tn
    row = lambda i, j: (i, 0)
    gate_w = lambda br: pl.BlockSpec((d, tn), lambda i, j: (0, br * nj + j))
    gate_b = lambda br: pl.BlockSpec((1, tn), lambda i, j: (0, br * nj + j))
    return pl.pallas_call(
        _merge_body,
        grid=(m // tm, nj),
        in_specs=[pl.BlockSpec((tm, d), row), gate_w(0), gate_w(1), gate_w(2),
                  gate_b(0), gate_b(1), gate_b(2),
                  pl.BlockSpec((tm, d_attn), row), pl.BlockSpec((d_attn, tn), lambda i, j: (0, j)),
                  pl.BlockSpec((tm, gc), lambda i, j: (i, j // per_group)),
                  pl.BlockSpec((gc, tn), lambda i, j: (j // per_group, j % per_group)),
                  pl.BlockSpec((1, tn), lambda i, j: (0, j)),
                  pl.BlockSpec((tm, d_mem), row), pl.BlockSpec((d_mem, tn), lambda i, j: (0, j))],
        out_specs=pl.BlockSpec((tm, tn), lambda i, j: (i, j)),
        out_shape=jax.ShapeDtypeStruct((m, d), BF16),
        compiler_params=_params(2, 48),
        name="merge",
    )(xb, w_gate, w_gate, w_gate, b_gate, b_gate, b_gate, attn, w_attn, mixed, w_pool, pool_scale,
      mem, w_mem)


def _layer_norm_rows(y, g, b):
    mu = jnp.mean(y, axis=-1, keepdims=True)
    yc = y - mu
    var = jnp.mean(yc * yc, axis=-1, keepdims=True)
    return yc * lax.rsqrt(var + LN_EPS) * g + b


def _outproj_body(a_ref, w_ref, x_ref, o_ref, *, alpha):
    o_ref[...] = alpha * x_ref[...] + _dot(a_ref[...], w_ref[...])


def _outproj_residual(merged, w_out, x, *, alpha):
    m, d = x.shape
    tm = _pick(m, (544, 512, 256, 128))
    tn = _pick(d, (1024, 512, 256, 128))
    return pl.pallas_call(
        functools.partial(_outproj_body, alpha=alpha),
        grid=(m // tm, d // tn),
        in_specs=[pl.BlockSpec((tm, d), lambda i, j: (i, 0)), pl.BlockSpec((d, tn), lambda i, j: (0, j)),
                  pl.BlockSpec((tm, tn), lambda i, j: (i, j))],
        out_specs=pl.BlockSpec((tm, tn), lambda i, j: (i, j)),
        out_shape=jax.ShapeDtypeStruct((m, d), F32),
        compiler_params=_params(2, 48),
        name="out_proj",
    )(merged, w_out, x)


def _pack_bf16_pairs(x):
    c = x.shape[1] // 2
    hi = pltpu.bitcast(x[:, :c].astype(BF16).astype(F32), jnp.uint32)
    lo = pltpu.bitcast(x[:, c:].astype(BF16).astype(F32), jnp.uint32)
    return hi | (lo >> 16)


def _unpack_bf16_pairs(w):
    hi = pltpu.bitcast(w & jnp.uint32(0xFFFF0000), F32).astype(BF16)
    lo = pltpu.bitcast(w << 16, F32).astype(BF16)
    return hi, lo


def _ln_router_body(y_ref, g_ref, b_ref, wr_ref, br_ref, x1_ref, xpk_ref, idx_ref, wgt_ref, *, n_exp):
    tm = y_ref.shape[0]
    xn = _layer_norm_rows(y_ref[...], g_ref[...], b_ref[...])
    x1_ref[...] = xn
    xpk_ref[...] = _pack_bf16_pairs(xn)
    logits = _dot(xn.astype(BF16), wr_ref[...].astype(BF16)) + br_ref[...]

    col = lax.broadcasted_iota(jnp.int32, (tm, n_exp), 1).astype(F32)
    vals, idxs = [], []
    for _ in range(TOP_K):
        mx = logits.max(axis=-1, keepdims=True)
        ix = jnp.where(logits == mx, col, float(n_exp)).min(axis=-1, keepdims=True)
        vals.append(mx)
        idxs.append(ix)
        logits = jnp.where(col == ix, -jnp.inf, logits)
    es = [jnp.exp(v - vals[0]) for v in vals]
    den = es[0]
    for e in es[1:]:
        den = den + e
    lane = lax.broadcasted_iota(jnp.int32, idx_ref.shape, 1)
    idx_out = jnp.zeros(idx_ref.shape, F32)
    wgt_out = jnp.zeros(wgt_ref.shape, F32)
    for k in range(TOP_K):
        idx_out = jnp.where(lane == k, idxs[k], idx_out)
        wgt_out = jnp.where(lane == k, es[k] / den, wgt_out)
    idx_ref[...] = idx_out.astype(jnp.int32)
    wgt_ref[...] = wgt_out


def _ln_router(y, ln_g, ln_b, w_router, b_router):
    m, d = y.shape
    n_exp = w_router.shape[1]
    tm = ROW_TILE
    row = lambda i: (i, 0)
    const = lambda i: (0, 0)
    return pl.pallas_call(
        functools.partial(_ln_router_body, n_exp=n_exp),
        grid=(m // tm,),
        in_specs=[pl.BlockSpec((tm, d), row), pl.BlockSpec((1, d), const), pl.BlockSpec((1, d), const),
                  pl.BlockSpec((d, n_exp), const), pl.BlockSpec((1, n_exp), const)],
        out_specs=[pl.BlockSpec((tm, d), row), pl.BlockSpec((tm, d // 2), row),
                   pl.BlockSpec((tm, 128), row), pl.BlockSpec((tm, 128), row)],
        out_shape=[jax.ShapeDtypeStruct((m, d), F32), jax.ShapeDtypeStruct((m, d // 2), jnp.uint32),
                   jax.ShapeDtypeStruct((m, 128), jnp.int32), jax.ShapeDtypeStruct((m, 128), F32)],
        compiler_params=_params(1, 40),
        name="ln_router",
    )(y, ln_g, ln_b, w_router, b_router)


def _route_tables(top_idx, n_exp):
    m, k = top_idx.shape
    i32 = jnp.int32
    e = top_idx.reshape(-1)
    onehot = (e[:, None] == jnp.arange(n_exp, dtype=i32)[None, :]).astype(i32)
    csum = jnp.cumsum(onehot, axis=0)
    rank = jnp.sum(csum * onehot, axis=1) - 1
    counts = csum[-1]
    ntile = (counts + ROW_TILE - 1) // ROW_TILE
    tend = jnp.cumsum(ntile)
    tstart = tend - ntile
    pos = jnp.sum(onehot * (tstart * ROW_TILE)[None, :], axis=1) + rank
    n_tiles = (m * k) // ROW_TILE + n_exp
    n_super = (m * k) // (ROW_TILE * SUPER_TILES) + n_exp

    nsup = (ntile + SUPER_TILES - 1) // SUPER_TILES
    send = jnp.cumsum(nsup)
    sstart = send - nsup
    six = jnp.arange(n_super, dtype=i32)
    se = jnp.minimum(jnp.sum((six[:, None] >= send[None, :]).astype(i32), axis=1), n_exp - 1)
    live = six < send[-1]
    se = jnp.where(live, se, se[jnp.maximum(send[-1] - 1, 0)])
    sub = (six - sstart[se]) * SUPER_TILES
    tile0 = jnp.where(live, tstart[se] + sub, 0)
    ntile_s = jnp.where(live, jnp.minimum(ntile[se] - sub, SUPER_TILES), 0)
    tables = dict(
        pos=pos.astype(i32), se=se.astype(i32), tile0=tile0.astype(i32), ntile_s=ntile_s.astype(i32),
        n_super_used=send[-1].reshape(1).astype(i32), n_tiles_used=tend[-1].reshape(1).astype(i32),
        valid_end=(tstart * ROW_TILE + counts).astype(i32), pad_end=(tend * ROW_TILE).astype(i32))
    return tables, n_tiles, n_super


def _rowtok_body(pos_ref, vend_ref, pend_ref, ntu_ref, rt_ref, *, n_pairs, n_rows, n_exp):
    def clear(i, c):
        rt_ref[i] = 0
        return c

    def clear_pad(e, c):
        return lax.fori_loop(vend_ref[e], pend_ref[e], clear, c)

    lax.fori_loop(0, n_exp, clear_pad, 0)
    lax.fori_loop(ntu_ref[0] * ROW_TILE, n_rows, clear, 0)

    def put(i, c):
        rt_ref[pos_ref[i]] = lax.div(i, TOP_K)
        return c

    lax.fori_loop(0, n_pairs, put, 0, unroll=8)


def _row_tokens(tables, n_rows, n_exp):
    pos = tables["pos"]
    body = functools.partial(_rowtok_body, n_pairs=pos.shape[0], n_rows=n_rows, n_exp=n_exp)
    smem = pl.BlockSpec(memory_space=pltpu.SMEM)
    return pl.pallas_call(
        body,
        in_specs=[smem, smem, smem, smem],
        out_specs=smem,
        out_shape=jax.ShapeDtypeStruct((n_rows,), jnp.int32),
        name="row_tokens",
    )(pos, tables["valid_end"], tables["pad_end"], tables["n_tiles_used"])


def _row_copy(src_hbm, src_row, dst, dst_row, sem):
    return pltpu.make_async_copy(src_hbm.at[pl.ds(src_row, 1), :], dst.at[pl.ds(dst_row, 1), :], sem)


def _gather_body(rt_ref, nu_ref, x_hbm, o_ref, buf, sem, *, tile):
    i = pl.program_id(0)
    nu = nu_ref[0]
    slot = lax.rem(i, 2)

    def issue(t, sl):
        def one(r, c):
            _row_copy(x_hbm, rt_ref[t * tile + r], buf.at[sl], r, sem.at[sl]).start()
            return c

        lax.fori_loop(0, tile, one, 0, unroll=8)

    def wait(sl):
        def one(r, c):
            _row_copy(x_hbm, 0, buf.at[sl], r, sem.at[sl]).wait()
            return c

        lax.fori_loop(0, tile, one, 0, unroll=8)

    @pl.when(i == 0)
    def _():
        issue(0, 0)

    @pl.when(i < nu)
    def _():
        wait(slot)

        @pl.when(i + 1 < nu)
        def _():
            issue(i + 1, 1 - slot)

        c = buf.shape[2]
        hi, lo = _unpack_bf16_pairs(buf[slot])
        o_ref[:, :c] = hi
        o_ref[:, c:] = lo

    @pl.when(i >= nu)
    def _():
        o_ref[...] = jnp.zeros(o_ref.shape, o_ref.dtype)


def _dispatch(xpk, row_tok, n_used, n_tiles, tile):
    m, half = xpk.shape
    d = 2 * half
    body = functools.partial(_gather_body, tile=tile)
    grid_spec = pltpu.PrefetchScalarGridSpec(
        num_scalar_prefetch=2,
        grid=(n_tiles,),
        in_specs=[pl.BlockSpec(memory_space=pl.ANY)],
        out_specs=pl.BlockSpec((tile, d), lambda i, rt, nu: (i, 0)),
        scratch_shapes=[pltpu.VMEM((2, tile, half), jnp.uint32), pltpu.SemaphoreType.DMA((2,))],
    )
    return pl.pallas_call(
        body,
        grid_spec=grid_spec,
        out_shape=jax.ShapeDtypeStruct((n_tiles * tile, d), BF16),
        compiler_params=_params(1, 32),
        name="moe_dispatch",
    )(row_tok, n_used, xpk)


def _grouped_body(se_ref, t0_ref, nt_ref, nsu_ref, ntu_ref, x_hbm, *rest,
                  n_w, tn, nj, n_super, n_tiles, epilogue):
    w_refs, b_refs = rest[:n_w], rest[n_w:2 * n_w]
    out_hbm = rest[2 * n_w]
    xres, wb, ostage, xsem, osem, flags = rest[2 * n_w + 1:]
    s, j = pl.program_id(0), pl.program_id(1)
    nsu = nsu_ref[0]
    live = s < nsu
    slot = lax.rem(s, 2)

    def rows(t):
        return pl.ds(pl.multiple_of(t * ROW_TILE, ROW_TILE), ROW_TILE)

    def lhs_copy(tile, q, sl):
        return pltpu.make_async_copy(x_hbm.at[rows(tile), :], xres.at[sl, rows(q), :], xsem.at[sl])

    def issue_lhs(sup, sl):
        def one(q, c):
            lhs_copy(t0_ref[sup] + q, q, sl).start()
            return c

        lax.fori_loop(0, nt_ref[sup], one, 0)

    def wait_lhs(sup, sl):
        def one(q, c):
            lhs_copy(0, q, sl).wait()
            return c

        lax.fori_loop(0, nt_ref[sup], one, 0)

    def out_copy(p, half, tile, col0):
        return pltpu.make_async_copy(ostage.at[p, pl.ds(half * ROW_TILE, ROW_TILE), :],
                                     out_hbm.at[rows(tile), pl.ds(col0, tn)], osem.at[p])

    def drain(p):
        def one(i, c):
            out_copy(p, 0, 0, 0).wait()
            return c

        lax.fori_loop(0, flags[p], one, 0)
        flags[p] = 0

    @pl.when((s == 0) & (j == 0))
    def _():
        flags[0] = 0
        flags[1] = 0
        issue_lhs(0, 0)

    @pl.when(live & (j == 0))
    def _():
        wait_lhs(s, slot)

        @pl.when(s + 1 < nsu)
        def _():
            issue_lhs(s + 1, 1 - slot)

    @pl.when(live)
    def _():
        for k, w_ref in enumerate(w_refs):
            wb[:, k * tn:(k + 1) * tn] = w_ref[...].astype(BF16)
        col0 = pl.multiple_of(j * tn, tn)

        nt = nt_ref[s]

        def emit(q, n_sub, p):
            n_row = n_sub * ROW_TILE
            x = xres[slot, pl.ds(pl.multiple_of(q * ROW_TILE, ROW_TILE), n_row), :]
            res = epilogue(_dot(x, wb[...]), [b[...] for b in b_refs])
            drain(p)
            ostage[p, 0:n_row, :] = res.astype(ostage.dtype)
            for half in range(n_sub):
                out_copy(p, half, t0_ref[s] + q + half, col0).start()
            flags[p] = n_sub

        def pair(i, c):
            emit(2 * i, 2, lax.rem(i, 2))
            return c

        n_pair = lax.div(nt, 2)
        lax.fori_loop(0, n_pair, pair, 0)

        @pl.when(lax.rem(nt, 2) == 1)
        def _():
            emit(nt - 1, 1, lax.rem(n_pair, 2))

    @pl.when((s == n_super - 1) & (j == nj - 1))
    def _():
        drain(0)
        drain(1)
        ostage[0] = jnp.zeros(ostage.shape[1:], ostage.dtype)

        def zero_tile(t, c):
            for jj in range(nj):
                out_copy(0, 0, t, jj * tn).start()
            for jj in range(nj):
                out_copy(0, 0, 0, 0).wait()
            return c

        lax.fori_loop(ntu_ref[0], n_tiles, zero_tile, 0)


def _grouped_gemm(xs, weights, biases, layer, col_blocks, tables, n_tiles, n_super, *,
                  tn, n_out, out_dtype, epilogue, name):
    n_rows, d = xs.shape
    n_w = len(weights)
    nj = n_out // tn

    def wmap(blk0):
        def index(s, j, se, t0, nt, nsu, ntu):
            return (layer, se[s], 0, blk0 + jnp.where(s < nsu[0], j, nj - 1))
        return index

    in_specs = [pl.BlockSpec(memory_space=pl.ANY)]
    in_specs += [pl.BlockSpec((None, None, d, tn), wmap(b0)) for b0 in col_blocks]
    in_specs += [pl.BlockSpec((None, None, 1, tn), wmap(b0)) for b0 in col_blocks]
    body = functools.partial(_grouped_body, n_w=n_w, tn=tn, nj=nj, n_super=n_super, n_tiles=n_tiles,
                             epilogue=epilogue)
    grid_spec = pltpu.PrefetchScalarGridSpec(
        num_scalar_prefetch=5,
        grid=(n_super, nj),
        in_specs=in_specs,
        out_specs=pl.BlockSpec(memory_space=pl.ANY),
        scratch_shapes=[pltpu.VMEM((2, SUPER_TILES * ROW_TILE, d), BF16),
                        pltpu.VMEM((d, n_w * tn), BF16),
                        pltpu.VMEM((2, 2 * ROW_TILE, tn), out_dtype),
                        pltpu.SemaphoreType.DMA((2,)), pltpu.SemaphoreType.DMA((2,)),
                        pltpu.SMEM((2,), jnp.int32)],
    )
    return pl.pallas_call(
        body,
        grid_spec=grid_spec,
        out_shape=jax.ShapeDtypeStruct((n_rows, n_out), out_dtype),
        compiler_params=_params(2, 56),
        name=name,
    )(tables["se"], tables["tile0"], tables["ntile_s"], tables["n_super_used"], tables["n_tiles_used"],
      xs, *weights, *biases)


def _swiglu_epilogue(h, biases):
    tn = h.shape[1] // 2
    gate = jnp.minimum(h[:, :tn] + biases[0], SWIGLU_LIMIT)
    up = jnp.clip(h[:, tn:] + biases[1], -SWIGLU_LIMIT, SWIGLU_LIMIT)
    glu = gate * jax.nn.sigmoid(SWIGLU_ALPHA * gate)
    return glu * (up + 1.0)


def _bias_epilogue(h, biases):
    return h + biases[0]


def _gemm1(xs, w_in, b_in, layer, tables, n_tiles, n_super):
    d_ff = w_in.shape[3] // 2
    tn = _pick(d_ff, (256, 128))
    return _grouped_gemm(xs, [w_in, w_in], [b_in, b_in], layer, [0, d_ff // tn], tables, n_tiles, n_super,
                         tn=tn, n_out=d_ff, out_dtype=BF16, epilogue=_swiglu_epilogue, name="moe_gemm1")


def _gemm2(act, w_out, b_out, layer, tables, n_tiles, n_super):
    d = w_out.shape[3]
    tn = _pick(d, (512, 256, 128))
    return _grouped_gemm(act, [w_out], [b_out], layer, [0], tables, n_tiles, n_super,
                         tn=tn, n_out=d, out_dtype=F32, epilogue=_bias_epilogue, name="moe_gemm2")


def _combine_body(pos_ref, y_hbm, x1_ref, w_ref, g_ref, b_ref, x2_ref, x2b_ref, buf, sem, *, tile, alpha):
    i = pl.program_id(0)
    slot = lax.rem(i, 2)

    def issue(t, sl):
        base = t * tile * TOP_K

        def one(r, c):
            for k in range(TOP_K):
                _row_copy(y_hbm, pos_ref[base + r * TOP_K + k], buf.at[sl, k], r, sem.at[sl]).start()
            return c

        lax.fori_loop(0, tile, one, 0, unroll=2)

    def wait(sl):
        def one(r, c):
            for k in range(TOP_K):
                _row_copy(y_hbm, 0, buf.at[sl, k], r, sem.at[sl]).wait()
            return c

        lax.fori_loop(0, tile, one, 0, unroll=2)

    @pl.when(i == 0)
    def _():
        issue(0, 0)

    wait(slot)

    @pl.when(i + 1 < pl.num_programs(0))
    def _():
        issue(i + 1, 1 - slot)

    moe = w_ref[:, 0:1] * buf[slot, 0]
    for k in range(1, TOP_K):
        moe = moe + w_ref[:, k:k + 1] * buf[slot, k]
    out = _layer_norm_rows(alpha * x1_ref[...] + moe, g_ref[...], b_ref[...])
    x2_ref[...] = out
    x2b_ref[...] = out.astype(BF16)


def _combine_ln(y, pos, x1, top_w, ln_g, ln_b, *, alpha, tile):
    m, d = x1.shape
    body = functools.partial(_combine_body, tile=tile, alpha=alpha)
    row = lambda i, pos: (i, 0)
    const = lambda i, pos: (0, 0)
    grid_spec = pltpu.PrefetchScalarGridSpec(
        num_scalar_prefetch=1,
        grid=(m // tile,),
        in_specs=[pl.BlockSpec(memory_space=pl.ANY), pl.BlockSpec((tile, d), row),
                  pl.BlockSpec((tile, 128), row), pl.BlockSpec((1, d), const), pl.BlockSpec((1, d), const)],
        out_specs=[pl.BlockSpec((tile, d), row), pl.BlockSpec((tile, d), row)],
        scratch_shapes=[pltpu.VMEM((2, TOP_K, tile, d), F32), pltpu.SemaphoreType.DMA((2,))],
    )
    return pl.pallas_call(
        body,
        grid_spec=grid_spec,
        out_shape=[jax.ShapeDtypeStruct((m, d), F32), jax.ShapeDtypeStruct((m, d), BF16)],
        compiler_params=_params(1, 52),
        name="moe_combine_ln",
    )(pos, y, x1, top_w, ln_g, ln_b)


def kernel(x_prompt, x_sample, mem_prompt, cache_band_k, cache_band_v, state_pool, cache_mem_k, cache_mem_v, w_in, rel_bias, w_attn_proj, w_pool_map, pool_scale, w_mem_kv, w_mem_proj, w_gate, b_gate, w_out, ln_mix_g, ln_mix_b, w_router, b_router, w_exp_in, b_exp_in, w_exp_out, b_exp_out, ln_ffn_g, ln_ffn_b):
    batch, seq, d = x_prompt.shape
    dbatch, dseq, _ = x_sample.shape
    depth = w_in.shape[0]
    n_mem = mem_prompt.shape[1]
    d_attn = w_attn_proj.shape[1]
    d_mem = w_mem_proj.shape[1]
    d_pool = w_pool_map.shape[1] * w_pool_map.shape[2]
    n_heads = d_attn // HEAD_DIM
    n_exp = w_router.shape[2]
    n_cached = cache_band_k.shape[2]
    mp, ms = batch * seq, dbatch * dseq
    alpha = (2 * depth) ** 0.25
    heads_per_step = _pick(n_heads, (4, 2, 1))
    qrows = ROW_TILE
    nkb = BAND_PREV * CHUNK // qrows + 1
    u_col_blk = (3 * d_attn) // d_pool
    qm_col_blk = (3 * d_attn + d_pool) // d_mem
    assert seq % qrows == 0 and mp % dseq == 0 and (mp + ms) % ROW_TILE == 0
    assert (3 * d_attn) % d_pool == 0 and (3 * d_attn + d_pool) % d_mem == 0

    x = jnp.concatenate([x_prompt.reshape(mp, d), x_sample.reshape(ms, d)], axis=0)
    xb = x.astype(BF16)
    mem_b = mem_prompt.reshape(batch * n_mem, d).astype(BF16)
    b_exp_in4 = b_exp_in.reshape(depth, n_exp, 1, -1)
    b_exp_out4 = b_exp_out.reshape(depth, n_exp, 1, -1)

    outs = {k: [] for k in ("bkp", "bvp", "pp", "mkp", "mvp", "bks", "bvs", "ps")}
    for l in range(depth):
        h = _matmul(xb, w_in[l].astype(BF16), F32, "in_proj")
        mem_kv = _matmul(mem_b, w_mem_kv[l].astype(BF16), F32, "mem_kv")

        bias_p = _band_bias_prompt(rel_bias[l], qrows, nkb)
        bias_c, bias_n = _band_bias_sample(rel_bias[l], dseq, n_cached)
        attn_p = _band_prompt(h, bias_p, batch=batch, seq=seq, d_attn=d_attn, heads_per_step=heads_per_step)
        attn_s = _band_sample(h, cache_band_k[l].reshape(dbatch, n_cached, d_attn),
                              cache_band_v[l].reshape(dbatch, n_cached, d_attn), bias_c, bias_n,
                              row0=mp, batch=dbatch, seq=dseq, d_attn=d_attn, heads_per_step=heads_per_step)
        attn = jnp.concatenate([attn_p, attn_s], axis=0)

        mem_p = _mem_attn(h, mem_kv, mem_kv, row0=0, batch=batch, seq=seq, d_mem=d_mem,
                          q_col_blk=qm_col_blk, k_col_blk=0, v_col_blk=1, n_mem=n_mem, name="mem_prompt")
        mem_s = _mem_attn(h, cache_mem_k[l].reshape(dbatch * n_mem, d_mem),
                          cache_mem_v[l].reshape(dbatch * n_mem, d_mem), row0=mp, batch=dbatch, seq=dseq,
                          d_mem=d_mem, q_col_blk=qm_col_blk, k_col_blk=0, v_col_blk=0, n_mem=n_mem,
                          name="mem_sample")
        mem = jnp.concatenate([mem_p, mem_s], axis=0)

        hist = jnp.pad(state_pool[l], ((0, 0), (HIST_ROWS - POOL_HIST, 0), (0, 0))).reshape(
            dbatch * HIST_ROWS, d_pool)
        mixed_p = _pool(h, None, row0=0, batch=batch, seq=seq, d_pool=d_pool, u_col_blk=u_col_blk, pos0=0,
                        name="pool_prompt")
        mixed_s = _pool(h, hist, row0=mp, batch=dbatch, seq=dseq, d_pool=d_pool, u_col_blk=u_col_blk,
                        pos0=PAST_LEN, name="pool_sample")
        mixed = jnp.concatenate([mixed_p, mixed_s], axis=0)

        merged = _merge(xb, w_gate[l].astype(BF16), b_gate[l].reshape(1, -1), attn,
                        w_attn_proj[l].astype(BF16), mixed,
                        w_pool_map[l].reshape(d_pool, -1).astype(BF16), pool_scale[l].reshape(1, -1),
                        mem, w_mem_proj[l].astype(BF16))
        pre_ln = _outproj_residual(merged, w_out[l].astype(BF16), x, alpha=alpha)
        x1, xpk, top_idx, top_w = _ln_router(pre_ln, ln_mix_g[l].reshape(1, -1), ln_mix_b[l].reshape(1, -1),
                                        w_router[l], b_router[l].reshape(1, -1))

        tables, n_tiles, n_super = _route_tables(top_idx[:, :TOP_K], n_exp)
        row_tok = _row_tokens(tables, n_tiles * ROW_TILE, n_exp)
        xs = _dispatch(xpk, row_tok, tables["n_tiles_used"], n_tiles, ROW_TILE)
        act = _gemm1(xs, w_exp_in, b_exp_in4, l, tables, n_tiles, n_super)
        y = _gemm2(act, w_exp_out, b_exp_out4, l, tables, n_tiles, n_super)
        x, xb = _combine_ln(y, tables["pos"], x1, top_w, ln_ffn_g[l].reshape(1, -1),
                            ln_ffn_b[l].reshape(1, -1), alpha=alpha, tile=COMBINE_TILE)

        hs = h[mp:].reshape(dbatch, dseq, -1)
        keep = min(BAND_PREV * CHUNK, seq)
        tail = max(keep, min(POOL_HIST, seq))
        hp = jnp.stack([h[(b + 1) * seq - tail:(b + 1) * seq, d_attn:3 * d_attn + d_pool] for b in range(batch)])
        outs["bkp"].append(hp[:, tail - keep:, :d_attn].reshape(batch, keep, n_heads, HEAD_DIM))
        outs["bvp"].append(hp[:, tail - keep:, d_attn:2 * d_attn].reshape(batch, keep, n_heads, HEAD_DIM))
        u_p = hp[:, :, 2 * d_attn:2 * d_attn + d_pool]
        outs["pp"].append(jnp.concatenate(
            [jnp.zeros((batch, POOL_HIST, d_pool), F32), u_p], axis=1)[:, -POOL_HIST:])
        hd_mem = d_mem // MEM_HEADS
        outs["mkp"].append(mem_kv[:, :d_mem].reshape(batch, n_mem, MEM_HEADS, hd_mem))
        outs["mvp"].append(mem_kv[:, d_mem:].reshape(batch, n_mem, MEM_HEADS, hd_mem))
        outs["bks"].append(hs[:, :, d_attn:2 * d_attn].reshape(dbatch, dseq, n_heads, HEAD_DIM))
        outs["bvs"].append(hs[:, :, 2 * d_attn:3 * d_attn].reshape(dbatch, dseq, n_heads, HEAD_DIM))
        u_s = hs[:, :, 3 * d_attn:3 * d_attn + d_pool]
        outs["ps"].append(jnp.concatenate([state_pool[l], u_s], axis=1)[:, -POOL_HIST:])

    st = lambda k: jnp.stack(outs[k])
    return (x[:mp].reshape(batch, seq, d), x[mp:].reshape(dbatch, dseq, d),
            st("bkp"), st("bvp"), st("pp"), st("mkp"), st("mvp"), st("bks"), st("bvs"), st("ps"))
```
